```python
import math
import jax, jax.numpy as jnp
from jax import lax
import numpy as np

D_MODEL = 1024
BATCH = 8
SEQ = 8192
DEPTH = 2
DEC_BATCH = 32
DEC_SEQ = 64
PAST_LEN = 1024

CHUNK = 64
D_MIX = D_MODEL
D_A = D_MIX // 4
D_B = D_MIX // 4
D_C = D_MIX - D_A - D_B
GDN_HEADS = 4
GDN_DK = D_C // GDN_HEADS
GDN_DV = D_C // GDN_HEADS
CONV_A = 3
CONV_B = 31
CONV_QKV = 4
D_FF = -(-(8 * D_MODEL) // (3 * 256)) * 256
D_IN = 3 * D_A + 2 * D_B + 4 * D_C + 2 * GDN_HEADS
EPS = 1e-6

kernel_name = 'hybrid_conv_deltanet_stream_step'


def _rmsnorm(x, g):
    x32 = x.astype(jnp.float32)
    y = x32 * lax.rsqrt(jnp.mean(x32 * x32, axis=-1, keepdims=True) + EPS)
    return (y * g.astype(jnp.float32)).astype(x.dtype)


def _layernorm(x, g, b):
    x32 = x.astype(jnp.float32)
    mu = jnp.mean(x32, axis=-1, keepdims=True)
    xc = x32 - mu
    y = xc * lax.rsqrt(jnp.mean(xc * xc, axis=-1, keepdims=True) + EPS)
    return (y * g.astype(jnp.float32) + b.astype(jnp.float32)).astype(x.dtype)


def _l2norm(t):
    return t * lax.rsqrt(jnp.sum(t * t, axis=-1, keepdims=True) + EPS)


def _causal_dwconv(u, buf, w):
    width = w.shape[0]
    padded = jnp.concatenate([buf.astype(u.dtype), u], axis=1)
    y = lax.conv_general_dilated(padded, w[:, None, :].astype(u.dtype), window_strides=(1,),
                                 padding='VALID', dimension_numbers=('NWC', 'WIO', 'NWC'),
                                 feature_group_count=u.shape[-1])
    return y, padded[:, -(width - 1):, :]


def _gdn_block(q, k, v, g, beta, S):
    L = q.shape[2]
    incl = jnp.tril(jnp.ones((L, L), dtype=bool))
    strict = jnp.tril(jnp.ones((L, L), dtype=bool), -1)
    gam = jnp.cumsum(g, axis=-1)
    decay = jnp.exp(jnp.where(incl, gam[..., :, None] - gam[..., None, :], -jnp.inf))
    kk = jnp.einsum('bhid,bhjd->bhij', k, k)
    a_low = jnp.where(strict, beta[..., :, None] * kk * decay, 0.0)
    eye = jnp.eye(L, dtype=q.dtype)
    rhs = jnp.concatenate([v * beta[..., None], k * (beta * jnp.exp(gam))[..., None]], axis=-1)
    sol = lax.linalg.triangular_solve(a_low + eye, rhs, left_side=True, lower=True)
    dv = v.shape[-1]
    u, w = sol[..., :dv], sol[..., dv:]
    v_new = u - jnp.einsum('bhlk,bhkv->bhlv', w, S)
    qk = jnp.einsum('bhid,bhjd->bhij', q, k) * decay
    o = (jnp.einsum('bhlk,bhkv->bhlv', q * jnp.exp(gam)[..., None], S)
         + jnp.einsum('bhij,bhjv->bhiv', qk, v_new))
    g_last = gam[..., -1:]
    S_new = (S * jnp.exp(g_last)[..., None]
             + jnp.einsum('bhlk,bhlv->bhkv', k * jnp.exp(g_last - gam)[..., None], v_new))
    return o, S_new


def _gdn(q, k, v, g, beta, S0):
    Bsz, H, T, _ = q.shape
    L = min(T, CHUNK)
    N = T // L

    def blocks(t):
        return jnp.moveaxis(t.reshape(t.shape[:2] + (N, L) + t.shape[3:]), 2, 0)

    def step(S, inp):
        o, S = _gdn_block(*inp, S)
        return S, o

    S, o = lax.scan(step, S0, (blocks(q), blocks(k), blocks(v), blocks(g), blocks(beta)))
    o = jnp.moveaxis(o, 0, 2).reshape(Bsz, H, T, -1)
    return o, S


def _mixer(h, buf_a, buf_b, buf_qkv, S0, p):
    Bsz, T, _ = h.shape
    f32 = jnp.float32
    z = h @ p['w_in']
    sizes = (D_A,) * 3 + (D_B,) * 2 + (D_C,) * 4 + (GDN_HEADS,) * 2
    cuts = np.cumsum(sizes)[:-1].tolist()
    h_a, b_a, c_a, p_a, p_g, q, k, v, z_g, a_dec, b_beta = jnp.split(z, cuts, axis=-1)
    conv_a, nbuf_a = _causal_dwconv(c_a * h_a, buf_a, p['conv_a_w'])
    y_a = b_a * conv_a
    glu = p_a * jax.nn.sigmoid(p_g)
    conv_b, nbuf_b = _causal_dwconv(glu, buf_b, p['conv_b_w'])
    y_b = jax.nn.silu(_layernorm(conv_b + p['conv_b_b'], p['ln_b_g'], p['ln_b_b']))
    qkv, nbuf_qkv = _causal_dwconv(jnp.concatenate([q, k, v], axis=-1), buf_qkv, p['conv_qkv_w'])
    qkv = jax.nn.silu(qkv).astype(f32)
    q, k, v = jnp.split(qkv, 3, axis=-1)
    heads = lambda t: t.reshape(Bsz, T, GDN_HEADS, -1).transpose(0, 2, 1, 3)
    q = _l2norm(heads(q)) * (GDN_DK ** -0.5)
    k = _l2norm(heads(k))
    v = heads(v)
    beta = jax.nn.sigmoid(b_beta.astype(f32)).transpose(0, 2, 1)
    g = (-jnp.exp(p['a_log'].astype(f32))
         * jax.nn.softplus(a_dec.astype(f32) + p['dt_bias'].astype(f32))).transpose(0, 2, 1)
    o, S = _gdn(q, k, v, g, beta, S0.astype(f32))
    o = o.transpose(0, 2, 1, 3)
    o = _rmsnorm(o, p['gdn_norm_g']) * jax.nn.silu(z_g.astype(f32).reshape(Bsz, T, GDN_HEADS, GDN_DV))
    y_c = o.reshape(Bsz, T, D_C).astype(h.dtype)
    y = jnp.concatenate([y_a, y_b, y_c], axis=-1) @ p['w_out']
    return y, nbuf_a, nbuf_b, nbuf_qkv, S.astype(h.dtype)


def _trunk(x, c, st_a, st_b, st_qkv, st_s, layers, final_norm_g):
    cmod = jax.nn.silu(c)
    new_a, new_b, new_q, new_s = [], [], [], []
    for l in range(DEPTH):
        p = {name: arr[l] for name, arr in layers.items()}
        mod = cmod @ p['w_ada'] + p['b_ada']
        sh1, sc1, g1, sh2, sc2, g2 = jnp.split(mod[:, None, :], 6, axis=-1)
        h = _rmsnorm(x, p['norm1_g']) * (1 + sc1) + sh1
        y, na, nb, nq, ns = _mixer(h, st_a[l], st_b[l], st_qkv[l], st_s[l], p)
        x = x + g1 * y
        h = _rmsnorm(x, p['norm2_g']) * (1 + sc2) + sh2
        gate, up = jnp.split(h @ p['w_gate_up'], 2, axis=-1)
        x = x + g2 * ((jax.nn.silu(gate) * up) @ p['w_down'])
        new_a.append(na)
        new_b.append(nb)
        new_q.append(nq)
        new_s.append(ns)
    return (_rmsnorm(x, final_norm_g), jnp.stack(new_a), jnp.stack(new_b),
            jnp.stack(new_q), jnp.stack(new_s))


def setup_inputs(seed: int = 0) -> dict:
    key = jax.random.key(seed)
    ks = jax.random.split(key, 32)
    f32 = jnp.float32

    def nrm(i, shape, scale=1.0):
        return scale * jax.random.normal(ks[i], shape, f32)

    dt = jnp.exp(jax.random.uniform(ks[19], (DEPTH, GDN_HEADS), f32, math.log(1e-3), math.log(1e-1)))
    return {
        'x_prompt': nrm(0, (BATCH, SEQ, D_MODEL)),
        'x_sample': nrm(1, (DEC_BATCH, DEC_SEQ, D_MODEL)),
        'state_conv_a': nrm(2, (DEPTH, DEC_BATCH, CONV_A - 1, D_A)),
        'state_conv_b': nrm(3, (DEPTH, DEC_BATCH, CONV_B - 1, D_B)),
        'state_conv_qkv': nrm(4, (DEPTH, DEC_BATCH, CONV_QKV - 1, 3 * D_C)),
        'state_gdn': nrm(5, (DEPTH, DEC_BATCH, GDN_HEADS, GDN_DK, GDN_DV), 0.1),
        'c_prompt': nrm(6, (BATCH, D_MODEL)),
        'c_sample': nrm(7, (DEC_BATCH, D_MODEL)),
        'norm1_g': 1.0 + nrm(8, (DEPTH, D_MODEL), 0.01),
        'w_ada': nrm(9, (DEPTH, D_MODEL, 6 * D_MODEL), 0.5 * D_MODEL ** -0.5),
        'b_ada': nrm(10, (DEPTH, 6 * D_MODEL), 0.01),
        'w_in': nrm(11, (DEPTH, D_MODEL, D_IN), D_MODEL ** -0.5),
        'conv_a_w': nrm(12, (DEPTH, CONV_A, D_A), CONV_A ** -0.5),
        'conv_b_w': nrm(13, (DEPTH, CONV_B, D_B), CONV_B ** -0.5),
        'conv_b_b': nrm(14, (DEPTH, D_B), 0.01),
        'ln_b_g': 1.0 + nrm(15, (DEPTH, D_B), 0.01),
        'ln_b_b': nrm(16, (DEPTH, D_B), 0.01),
        'conv_qkv_w': nrm(17, (DEPTH, CONV_QKV, 3 * D_C), CONV_QKV ** -0.5),
        'a_log': jnp.log(jax.random.uniform(ks[18], (DEPTH, GDN_HEADS), f32, 1.0, 16.0)),
        'dt_bias': dt + jnp.log(-jnp.expm1(-dt)),
        'gdn_norm_g': 1.0 + nrm(20, (DEPTH, GDN_DV), 0.01),
        'w_out': nrm(21, (DEPTH, D_MIX, D_MODEL), D_MIX ** -0.5),
        'norm2_g': 1.0 + nrm(22, (DEPTH, D_MODEL), 0.01),
        'w_gate_up': nrm(23, (DEPTH, D_MODEL, 2 * D_FF), D_MODEL ** -0.5),
        'w_down': nrm(24, (DEPTH, D_FF, D_MODEL), D_FF ** -0.5),
        'final_norm_g': 1.0 + nrm(25, (D_MODEL,), 0.01),
    }


def reference(x_prompt, x_sample, state_conv_a, state_conv_b, state_conv_qkv, state_gdn,
              c_prompt, c_sample, norm1_g, w_ada, b_ada, w_in, conv_a_w, conv_b_w, conv_b_b,
              ln_b_g, ln_b_b, conv_qkv_w, a_log, dt_bias, gdn_norm_g, w_out, norm2_g,
              w_gate_up, w_down, final_norm_g):
    layers = {
        'norm1_g': norm1_g, 'w_ada': w_ada, 'b_ada': b_ada, 'w_in': w_in,
        'conv_a_w': conv_a_w, 'conv_b_w': conv_b_w, 'conv_b_b': conv_b_b,
        'ln_b_g': ln_b_g, 'ln_b_b': ln_b_b, 'conv_qkv_w': conv_qkv_w,
        'a_log': a_log, 'dt_bias': dt_bias, 'gdn_norm_g': gdn_norm_g, 'w_out': w_out,
        'norm2_g': norm2_g, 'w_gate_up': w_gate_up, 'w_down': w_down,
    }
    bp = x_prompt.shape[0]
    dtp = x_prompt.dtype
    y_prompt, pa, pb, pq, ps = _trunk(
        x_prompt, c_prompt,
        jnp.zeros((DEPTH, bp, CONV_A - 1, D_A), dtp),
        jnp.zeros((DEPTH, bp, CONV_B - 1, D_B), dtp),
        jnp.zeros((DEPTH, bp, CONV_QKV - 1, 3 * D_C), dtp),
        jnp.zeros((DEPTH, bp, GDN_HEADS, GDN_DK, GDN_DV), dtp),
        layers, final_norm_g)
    y_sample, sa, sb, sq, ss = _trunk(
        x_sample, c_sample, state_conv_a, state_conv_b, state_conv_qkv, state_gdn,
        layers, final_norm_g)
    return (y_prompt, y_sample, pa, pb, pq, ps, sa, sb, sq, ss)
```

```python
import functools

import jax
import jax.numpy as jnp
from jax import lax
from jax.experimental import pallas as pl
from jax.experimental.pallas import tpu as pltpu

F32 = jnp.float32
BF16 = jnp.bfloat16
EPS = 1e-6
CHUNK = 64
LANES = 128
HALO_A, HALO_B, HALO_Q = 8, 32, 8
VMEM_LIMIT_BYTES = 56 * 1024 * 1024
MAX_TILE_ROWS = 512

NN = ((1,), (0,))
NT = ((1,), (1,))
TN = ((0,), (0,))


def _sigmoid(x):
    return 1.0 / (1.0 + jnp.exp(-x))


def _silu(x):
    return x * _sigmoid(x)


def _softplus(x):
    return jnp.maximum(x, 0.0) + jnp.log1p(jnp.exp(-jnp.abs(x)))


def _dot(a, b, dims=NN):
    return lax.dot_general(a, b, (dims, ((), ())), preferred_element_type=F32)


def _split(a):
    hi = a.astype(BF16)
    lo = (a - hi.astype(F32)).astype(BF16)
    return hi, lo


def _mm3(a, b, dims=NN):
    ah, al = _split(a)
    bh, bl = _split(b)
    return _dot(ah, bh, dims) + (_dot(ah, bl, dims) + _dot(al, bh, dims))


def _rms_scale(x):
    return lax.rsqrt(jnp.mean(x * x, axis=-1, keepdims=True) + EPS)


def _cumsum_rows(x, row_idx):
    k = 1
    while k < x.shape[0]:
        x = x + jnp.where(row_idx >= k, pltpu.roll(x, k, axis=0), 0.0)
        k *= 2
    return x


def _ada_kernel(c_ref, w_ref, b_ref, o_ref):
    cm = _silu(c_ref[...])
    o_ref[0] = _dot(cm.astype(BF16), w_ref[0].astype(BF16)) + b_ref[0]


def _ada_call(c_all, w_ada, b_ada):
    depth, d, n6 = w_ada.shape
    rows = c_all.shape[0]
    tn = n6 // 4
    return pl.pallas_call(
        _ada_kernel,
        grid=(depth, n6 // tn),
        in_specs=[
            pl.BlockSpec((rows, d), lambda l, j: (0, 0)),
            pl.BlockSpec((1, d, tn), lambda l, j: (l, 0, j)),
            pl.BlockSpec((1, 1, tn), lambda l, j: (l, 0, j)),
        ],
        out_specs=pl.BlockSpec((1, rows, tn), lambda l, j: (l, 0, j)),
        out_shape=jax.ShapeDtypeStruct((depth, rows, n6), F32),
        compiler_params=pltpu.CompilerParams(
            dimension_semantics=("arbitrary", "arbitrary"), vmem_limit_bytes=VMEM_LIMIT_BYTES),
        name="ada_mod",
    )(c_all, w_ada, b_ada.reshape(depth, 1, n6))


def _tri_inverse(a, masks):
    eye, m8, m16, m32 = masks
    a8 = jnp.where(m8, a, 0.0)
    a2 = _mm3(a8, a8)
    a4 = _mm3(a2, a2)
    p = eye - a8
    p = p + _mm3(p, a2)
    x = p + _mm3(p, a4)
    prev = m8
    for m in (m16, m32, None):
        e = jnp.where(prev, 0.0, a) if m is None else jnp.where(jnp.logical_and(m, jnp.logical_not(prev)), a, 0.0)
        x = x - _mm3(x, _mm3(e, x))
        prev = m
    return x


def _mixer_kernel(dims, x_ref, mod_ref, n1g_ref, win_ref, caw_ref, cbw_ref, cbb_ref, lng_ref, lnb_ref,
                  cqw_ref, alog_ref, dtb_ref, gng_ref, wout_ref, sta_ref, stb_ref, stq_ref, sts_ref,
                  o_ref, na_ref, nb_ref, nq_ref, ns_ref,
                  h_scr, abuf, ba_scr, bbuf, qbuf, zg_scr, gb_scr, ymix_scr):
    sb, tt, d, d_a, d_b, d_c, heads, taps_a, taps_b, taps_q = dims
    dk = d_c // heads
    o_b = 3 * d_a
    o_q = o_b + 2 * d_b
    o_g = o_q + 3 * d_c
    o_s = o_g + d_c
    t = pl.program_id(1)
    n_t = pl.num_programs(1)

    @pl.when(t == 0)
    def _load_state():
        abuf[:, HALO_A - (taps_a - 1):HALO_A, :] = sta_ref[...]
        bbuf[:, HALO_B - (taps_b - 1):HALO_B, :] = stb_ref[...]
        qbuf[:, HALO_Q - (taps_q - 1):HALO_Q, :] = stq_ref[...]
        ns_ref[...] = sts_ref[...]

    n1g = n1g_ref[...]
    for s in range(sb):
        x = x_ref[s]
        m = mod_ref[s]
        h = (x * _rms_scale(x) * n1g) * (1.0 + m[:, d:2 * d]) + m[:, 0:d]
        h_scr[s * tt:(s + 1) * tt, :] = h.astype(BF16)
    hb = h_scr[...]
    z_a = _dot(hb, win_ref[:, 0:o_b])
    z_b = _dot(hb, win_ref[:, o_b:o_q])
    z_q = _dot(hb, win_ref[:, o_q:o_g])
    zg_scr[...] = _dot(hb, win_ref[:, o_g:o_s])
    z_s = _dot(hb, win_ref[:, o_s:o_s + LANES])
    ba_scr[...] = z_a[:, d_a:2 * d_a]
    u_a = z_a[:, 2 * d_a:3 * d_a] * z_a[:, 0:d_a]
    glu = z_b[:, 0:d_b] * _sigmoid(z_b[:, d_b:2 * d_b])
    for s in range(sb):
        abuf[s, HALO_A:HALO_A + tt, :] = u_a[s * tt:(s + 1) * tt]
        bbuf[s, HALO_B:HALO_B + tt, :] = glu[s * tt:(s + 1) * tt]
        qbuf[s, HALO_Q:HALO_Q + tt, :] = z_q[s * tt:(s + 1) * tt]
    lane = lax.broadcasted_iota(jnp.int32, z_s.shape, 1)
    g_all = -jnp.exp(alog_ref[...]) * _softplus(z_s + dtb_ref[...])
    gb_scr[...] = jnp.where(lane < heads, g_all, _sigmoid(z_s))

    ri = lax.broadcasted_iota(jnp.int32, (CHUNK, CHUNK), 0)
    ci = lax.broadcasted_iota(jnp.int32, (CHUNK, CHUNK), 1)
    incl = ri >= ci
    strict = ri > ci
    eye = jnp.where(ri == ci, 1.0, 0.0).astype(F32)
    masks = (eye,) + tuple((ri // b) == (ci // b) for b in (8, 16, 32))
    row_idx = lax.broadcasted_iota(jnp.int32, (CHUNK, LANES), 0)
    is_g_lane = lax.broadcasted_iota(jnp.int32, (CHUNK, LANES), 1) < heads
    cpt = tt // CHUNK
    caw = caw_ref[...]
    cbw = cbw_ref[...]
    cqw = cqw_ref[...]
    gng = gng_ref[...]

    def chunk_body(i, carry):
        if sb == 1:
            s, c = 0, i
        elif cpt == 1:
            s, c = i, 0
        else:
            s, c = i // cpt, i % cpt
        r0 = pl.multiple_of(c * CHUNK, CHUNK)
        row = pl.multiple_of(s * tt + c * CHUNK, CHUNK)

        win = abuf[s, pl.ds(r0, CHUNK + HALO_A), :]
        conv = caw[0:1] * win[HALO_A - taps_a + 1:HALO_A - taps_a + 1 + CHUNK]
        for j in range(1, taps_a):
            off = HALO_A - taps_a + 1 + j
            conv = conv + caw[j:j + 1] * win[off:off + CHUNK]
        ymix_scr[pl.ds(row, CHUNK), 0:d_a] = (ba_scr[pl.ds(row, CHUNK), :] * conv).astype(BF16)

        win = bbuf[s, pl.ds(r0, CHUNK + HALO_B), :]
        conv = cbw[0:1] * win[HALO_B - taps_b + 1:HALO_B - taps_b + 1 + CHUNK]
        for j in range(1, taps_b):
            off = HALO_B - taps_b + 1 + j
            conv = conv + cbw[j:j + 1] * win[off:off + CHUNK]
        conv = conv + cbb_ref[...]
        xc = conv - jnp.mean(conv, axis=-1, keepdims=True)
        ln = xc * lax.rsqrt(jnp.mean(xc * xc, axis=-1, keepdims=True) + EPS) * lng_ref[...] + lnb_ref[...]
        ymix_scr[pl.ds(row, CHUNK), d_a:d_a + d_b] = _silu(ln).astype(BF16)

        win = qbuf[s, pl.ds(r0, CHUNK + HALO_Q), :]
        conv = cqw[0:1] * win[HALO_Q - taps_q + 1:HALO_Q - taps_q + 1 + CHUNK]
        for j in range(1, taps_q):
            off = HALO_Q - taps_q + 1 + j
            conv = conv + cqw[j:j + 1] * win[off:off + CHUNK]
        qkv = _silu(conv)
        gb = gb_scr[pl.ds(row, CHUNK), :]
        gam = _cumsum_rows(jnp.where(is_g_lane, gb, 0.0), row_idx)
        gam_t = gam.T
        zg = zg_scr[pl.ds(row, CHUNK), :]
        for hd in range(heads):
            q = qkv[:, hd * dk:(hd + 1) * dk]
            k = qkv[:, d_c + hd * dk:d_c + (hd + 1) * dk]
            v = qkv[:, 2 * d_c + hd * dk:2 * d_c + (hd + 1) * dk]
            q = q * (lax.rsqrt(jnp.sum(q * q, axis=-1, keepdims=True) + EPS) * (dk ** -0.5))
            k = k * lax.rsqrt(jnp.sum(k * k, axis=-1, keepdims=True) + EPS)
            g_col = gam[:, hd:hd + 1]
            g_row = gam_t[hd:hd + 1, :]
            beta = gb[:, heads + hd:heads + hd + 1]
            decay = jnp.where(incl, jnp.exp(jnp.minimum(g_col - g_row, 0.0)), 0.0)
            eg = jnp.exp(g_col)
            a_low = jnp.where(strict, beta * _mm3(k, k, NT) * decay, 0.0)
            t_inv = _tri_inverse(a_low, masks)
            rhs = jnp.concatenate([v * beta, k * (beta * eg)], axis=-1)
            sol = _mm3(t_inv, rhs)
            s_old = ns_ref[s, hd]
            v_new = sol[:, 0:dk] - _mm3(sol[:, dk:2 * dk], s_old)
            qk = _mm3(q, k, NT) * decay
            o = _mm3(q * eg, s_old) + _mm3(qk, v_new)
            g_last = gam[CHUNK - 1:CHUNK, hd:hd + 1]
            ns_ref[s, hd] = s_old * jnp.exp(g_last) + _mm3(k * jnp.exp(g_last - g_col), v_new, TN)
            o = o * _rms_scale(o) * gng
            y_c = o * _silu(zg[:, hd * dk:(hd + 1) * dk])
            ymix_scr[pl.ds(row, CHUNK), d_a + d_b + hd * dk:d_a + d_b + (hd + 1) * dk] = y_c.astype(BF16)
        return carry

    lax.fori_loop(0, sb * cpt, chunk_body, 0)

    y = _dot(ymix_scr[...], wout_ref[...])
    for s in range(sb):
        g1 = mod_ref[s][:, 2 * d:3 * d]
        o_ref[s] = x_ref[s] + g1 * y[s * tt:(s + 1) * tt]

    @pl.when(t == n_t - 1)
    def _store_state():
        na_ref[...] = abuf[:, tt + HALO_A - (taps_a - 1):tt + HALO_A, :]
        nb_ref[...] = bbuf[:, tt + HALO_B - (taps_b - 1):tt + HALO_B, :]
        nq_ref[...] = qbuf[:, tt + HALO_Q - (taps_q - 1):tt + HALO_Q, :]

    abuf[:, 0:HALO_A, :] = abuf[:, tt:tt + HALO_A, :]
    bbuf[:, 0:HALO_B, :] = bbuf[:, tt:tt + HALO_B, :]
    qbuf[:, 0:HALO_Q, :] = qbuf[:, tt:tt + HALO_Q, :]


def _tiles(n_streams, t_len):
    assert t_len % CHUNK == 0
    if t_len >= MAX_TILE_ROWS:
        assert t_len % MAX_TILE_ROWS == 0
        return 1, MAX_TILE_ROWS
    sb = min(n_streams, MAX_TILE_ROWS // t_len)
    while n_streams % sb:
        sb -= 1
    return sb, t_len


def _full(shape):
    return pl.BlockSpec(shape, lambda b, t: (0,) * len(shape))


def _mixer_call(x, mod, p, st_a, st_b, st_q, st_s):
    n_b, t_len, d = x.shape
    d_a, d_b, d_c3 = p["conv_a_w"].shape[-1], p["conv_b_w"].shape[-1], p["conv_qkv_w"].shape[-1]
    d_c = d_c3 // 3
    heads = st_s.shape[1]
    taps_a, taps_b, taps_q = p["conv_a_w"].shape[0], p["conv_b_w"].shape[0], p["conv_qkv_w"].shape[0]
    assert taps_a - 1 <= HALO_A and taps_b - 1 <= HALO_B and taps_q - 1 <= HALO_Q
    assert d_c // heads == LANES and 2 * heads <= LANES
    sb, tt = _tiles(n_b, t_len)
    rows = sb * tt
    dims = (sb, tt, d, d_a, d_b, d_c, heads, taps_a, taps_b, taps_q)
    win = p["w_in_pad"]

    def per_stream(shape):
        return pl.BlockSpec((sb,) + shape, lambda b, t: (b,) + (0,) * len(shape))

    x_spec = pl.BlockSpec((sb, tt, d), lambda b, t: (b, t, 0))
    state_shapes = [st_a.shape[1:], st_b.shape[1:], st_q.shape[1:], st_s.shape[1:]]
    in_specs = [x_spec, per_stream((1, mod.shape[-1])), _full((1, d)), _full(win.shape),
                _full(p["conv_a_w"].shape), _full(p["conv_b_w"].shape), _full((1, d_b)), _full((1, d_b)),
                _full((1, d_b)), _full(p["conv_qkv_w"].shape), _full((1, LANES)), _full((1, LANES)),
                _full((1, LANES)), _full(p["w_out_bf"].shape)] + [per_stream(s) for s in state_shapes]
    out_specs = [x_spec] + [per_stream(s) for s in state_shapes]
    out_shape = [jax.ShapeDtypeStruct(x.shape, F32)] + [jax.ShapeDtypeStruct(a.shape, F32) for a in (st_a, st_b, st_q, st_s)]
    scratch = [
        pltpu.VMEM((rows, d), BF16),
        pltpu.VMEM((sb, tt + HALO_A, d_a), F32),
        pltpu.VMEM((rows, d_a), F32),
        pltpu.VMEM((sb, tt + HALO_B, d_b), F32),
        pltpu.VMEM((sb, tt + HALO_Q, d_c3), F32),
        pltpu.VMEM((rows, d_c), F32),
        pltpu.VMEM((rows, LANES), F32),
        pltpu.VMEM((rows, d), BF16),
    ]
    return pl.pallas_call(
        functools.partial(_mixer_kernel, dims),
        grid=(n_b // sb, t_len // tt),
        in_specs=in_specs,
        out_specs=out_specs,
        out_shape=out_shape,
        scratch_shapes=scratch,
        compiler_params=pltpu.CompilerParams(
            dimension_semantics=("arbitrary", "arbitrary"), vmem_limit_bytes=VMEM_LIMIT_BYTES),
        name="mixer",
    )(x, mod, p["norm1_g"], win, p["conv_a_w"], p["conv_b_w"], p["conv_b_b"], p["ln_b_g"], p["ln_b_b"],
      p["conv_qkv_w"], p["a_log_pad"], p["dt_bias_pad"], p["gdn_norm_g"], p["w_out_bf"], st_a, st_b, st_q, st_s)


def _ffn_kernel(dims, x_ref, mod_ref, n2g_ref, wgu_ref, wd_ref, fg_ref, o_ref, h_scr):
    sb, tt, d, ff, ff_chunks, final_norm = dims
    n2g = n2g_ref[...]
    for s in range(sb):
        x = x_ref[s]
        m = mod_ref[s]
        h = (x * _rms_scale(x) * n2g) * (1.0 + m[:, 4 * d:5 * d]) + m[:, 3 * d:4 * d]
        h_scr[s * tt:(s + 1) * tt, :] = h.astype(BF16)
    hb = h_scr[...]
    acc = None
    for c0, c1 in ff_chunks:
        gate = _dot(hb, wgu_ref[:, c0:c1])
        up = _dot(hb, wgu_ref[:, ff + c0:ff + c1])
        part = _dot((_silu(gate) * up).astype(BF16), wd_ref[c0:c1, :])
        acc = part if acc is None else acc + part
    for s in range(sb):
        out = x_ref[s] + mod_ref[s][:, 5 * d:6 * d] * acc[s * tt:(s + 1) * tt]
        if final_norm:
            out = out * _rms_scale(out) * fg_ref[...]
        o_ref[s] = out


def _ffn_call(x, mod, p, final_g, final_norm):
    n_b, t_len, d = x.shape
    ff = p["w_down_bf"].shape[0]
    sb, tt = _tiles(n_b, t_len)
    n_chunks = -(-ff // 1024)
    step = -(-ff // (n_chunks * 2 * LANES)) * 2 * LANES
    ff_chunks = tuple((c0, min(c0 + step, ff)) for c0 in range(0, ff, step))
    dims = (sb, tt, d, ff, ff_chunks, final_norm)
    x_spec = pl.BlockSpec((sb, tt, d), lambda b, t: (b, t, 0))
    return pl.pallas_call(
        functools.partial(_ffn_kernel, dims),
        grid=(n_b // sb, t_len // tt),
        in_specs=[x_spec, pl.BlockSpec((sb, 1, mod.shape[-1]), lambda b, t: (b, 0, 0)), _full((1, d)),
                  _full(p["w_gate_up_bf"].shape), _full(p["w_down_bf"].shape), _full((1, d))],
        out_specs=x_spec,
        out_shape=jax.ShapeDtypeStruct(x.shape, F32),
        scratch_shapes=[pltpu.VMEM((sb * tt, d), BF16)],
        compiler_params=pltpu.CompilerParams(
            dimension_semantics=("arbitrary", "arbitrary"), vmem_limit_bytes=VMEM_LIMIT_BYTES),
        name="ffn",
    )(x, mod, p["norm2_g"], p["w_gate_up_bf"], p["w_down_bf"], final_g)


def _layer_params(l, heads, norm1_g, w_in, conv_a_w, conv_b_w, conv_b_b, ln_b_g, ln_b_b, conv_qkv_w, a_log,
                  dt_bias, gdn_norm_g, w_out, norm2_g, w_gate_up, w_down):
    d, d_in = w_in.shape[1:]
    pad_small = lambda v: jnp.pad(v[l], (0, LANES - heads)).reshape(1, LANES)
    return {
        "norm1_g": norm1_g[l].reshape(1, d),
        "w_in_pad": jnp.pad(w_in[l], ((0, 0), (0, LANES - 2 * heads))).astype(BF16),
        "conv_a_w": conv_a_w[l], "conv_b_w": conv_b_w[l], "conv_b_b": conv_b_b[l].reshape(1, -1),
        "ln_b_g": ln_b_g[l].reshape(1, -1), "ln_b_b": ln_b_b[l].reshape(1, -1), "conv_qkv_w": conv_qkv_w[l],
        "a_log_pad": pad_small(a_log), "dt_bias_pad": pad_small(dt_bias),
        "gdn_norm_g": gdn_norm_g[l].reshape(1, -1), "w_out_bf": w_out[l].astype(BF16),
        "norm2_g": norm2_g[l].reshape(1, d), "w_gate_up_bf": w_gate_up[l].astype(BF16),
        "w_down_bf": w_down[l].astype(BF16),
    }


def _trunk(x, mods, layers, states, final_g):
    depth = len(layers)
    new = [[], [], [], []]
    for l, p in enumerate(layers):
        mod = mods[l][:, None, :]
        outs = _mixer_call(x, mod, p, *(st[l] for st in states))
        x = _ffn_call(outs[0], mod, p, final_g, l == depth - 1)
        for acc, o in zip(new, outs[1:]):
            acc.append(o)
    return (x,) + tuple(jnp.stack(n) for n in new)


def kernel(x_prompt, x_sample, state_conv_a, state_conv_b, state_conv_qkv, state_gdn, c_prompt, c_sample, norm1_g, w_ada, b_ada, w_in, conv_a_w, conv_b_w, conv_b_b, ln_b_g, ln_b_b, conv_qkv_w, a_log, dt_bias, gdn_norm_g, w_out, norm2_g, w_gate_up, w_down, final_norm_g):
    depth = w_in.shape[0]
    heads = a_log.shape[-1]
    bp = x_prompt.shape[0]
    layers = [_layer_params(l, heads, norm1_g, w_in, conv_a_w, conv_b_w, conv_b_b, ln_b_g, ln_b_b, conv_qkv_w,
                            a_log, dt_bias, gdn_norm_g, w_out, norm2_g, w_gate_up, w_down) for l in range(depth)]
    mods = _ada_call(jnp.concatenate([c_prompt, c_sample], axis=0), w_ada, b_ada)
    final_g = final_norm_g.reshape(1, -1)
    zero_states = tuple(jnp.zeros((depth, bp) + st.shape[2:], F32)
                        for st in (state_conv_a, state_conv_b, state_conv_qkv, state_gdn))
    y_p, pa, pb, pq, ps = _trunk(x_prompt, mods[:, :bp], layers, zero_states, final_g)
    y_s, sa, sb, sq, ss = _trunk(x_sample, mods[:, bp:], layers,
                                 (state_conv_a, state_conv_b, state_conv_qkv, state_gdn), final_g)
    return (y_p, y_s, pa, pb, pq, ps, sa, sb, sq, ss)
```

```python
import functools

import jax
import jax.numpy as jnp
from jax import lax
from jax.experimental import pallas as pl
from jax.experimental.pallas import tpu as pltpu

F32 = jnp.float32
BF16 = jnp.bfloat16
EPS = 1e-6
CHUNK = 64
LANES = 128
SUBLANES = 8
HALO_A, HALO_B, HALO_Q = 8, 32, 8
VMEM_LIMIT_BYTES = 56 * 1024 * 1024
MAX_TILE_ROWS = 512
STREAMS_PER_STEP = 2

NN = ((1,), (0,))
NT = ((1,), (1,))
TN = ((0,), (0,))


def _sigmoid(x):
    return 0.5 * jnp.tanh(0.5 * x) + 0.5


def _silu(x):
    return x * _sigmoid(x)


def _softplus(x):
    return jnp.maximum(x, 0.0) + jnp.log1p(jnp.exp(-jnp.abs(x)))


def _dot(a, b, dims=NN):
    return lax.dot_general(a, b, (dims, ((), ())), preferred_element_type=F32)


def _bdot(a, b, dims=NN):
    return _dot(a.astype(BF16), b.astype(BF16), dims)


def _causal_taps(win, w, halo):
    taps, n = w.shape[0], win.shape[0]
    first = halo - taps + 1
    acc = None
    for r in range(SUBLANES):
        offs = [o for o in range(first, halo + 1) if o % SUBLANES == r]
        if not offs:
            continue
        wr = win if r == 0 else pltpu.roll(win, n - r, axis=0)
        for o in offs:
            term = w[o - first:o - first + 1] * wr[o - r:o - r + CHUNK]
            acc = term if acc is None else acc + term
    return acc


def _rms_scale(x):
    return lax.rsqrt(jnp.mean(x * x, axis=-1, keepdims=True) + EPS)


def _cumsum_rows(x, row_idx):
    k = 1
    while k < x.shape[0]:
        x = x + jnp.where(row_idx >= k, pltpu.roll(x, k, axis=0), 0.0)
        k *= 2
    return x


def _ada_kernel(c_ref, w_ref, b_ref, o_ref):
    cm = _silu(c_ref[...])
    o_ref[0] = _dot(cm.astype(BF16), w_ref[0].astype(BF16)) + b_ref[0]


def _ada_call(c_all, w_ada, b_ada):
    depth, d, n6 = w_ada.shape
    rows = c_all.shape[0]
    tn = n6 // 4
    return pl.pallas_call(
        _ada_kernel,
        grid=(depth, n6 // tn),
        in_specs=[
            pl.BlockSpec((rows, d), lambda l, j: (0, 0)),
            pl.BlockSpec((1, d, tn), lambda l, j: (l, 0, j)),
            pl.BlockSpec((1, 1, tn), lambda l, j: (l, 0, j)),
        ],
        out_specs=pl.BlockSpec((1, rows, tn), lambda l, j: (l, 0, j)),
        out_shape=jax.ShapeDtypeStruct((depth, rows, n6), F32),
        compiler_params=pltpu.CompilerParams(
            dimension_semantics=("arbitrary", "arbitrary"), vmem_limit_bytes=VMEM_LIMIT_BYTES),
        name="ada_mod",
    )(c_all, w_ada, b_ada.reshape(depth, 1, n6))


def _tri_inverse(mats, masks):
    eye, m8, m16, m32 = masks
    a8 = [jnp.where(m8, a, 0.0) for a in mats]
    a2b = [_bdot(m, m).astype(BF16) for m in a8]
    a4 = [_dot(b, b).astype(BF16) for b in a2b]
    p = [eye - m for m in a8]
    p = [m + _dot(m.astype(BF16), b) for m, b in zip(p, a2b)]
    x = [m + _dot(m.astype(BF16), b) for m, b in zip(p, a4)]
    prev = m8
    for m in (m16, m32, None):
        sel = jnp.logical_not(prev) if m is None else jnp.logical_and(m, jnp.logical_not(prev))
        xb = [v.astype(BF16) for v in x]
        ex = [_dot(jnp.where(sel, a, 0.0).astype(BF16), b).astype(BF16) for a, b in zip(mats, xb)]
        x = [v - _dot(b, e) for v, b, e in zip(x, xb, ex)]
        prev = m
    return x


def _mixer_kernel(dims, x_ref, mod_ref, n1g_ref, win_ref, caw_ref, cbw_ref, cbb_ref, lng_ref, lnb_ref,
                  cqw_ref, alog_ref, dtb_ref, gng_ref, wout_ref, sta_ref, stb_ref, stq_ref, sts_ref,
                  o_ref, na_ref, nb_ref, nq_ref, ns_ref,
                  h_scr, abuf, ba_scr, bbuf, qbuf, zg_scr, gb_scr, ymix_scr):
    sb, tt, d, d_a, d_b, d_c, heads, taps_a, taps_b, taps_q = dims
    dk = d_c // heads
    o_b = 3 * d_a
    o_q = o_b + 2 * d_b
    o_g = o_q + 3 * d_c
    o_s = o_g + d_c
    t = pl.program_id(1)
    n_t = pl.num_programs(1)

    @pl.when(t == 0)
    def _load_state():
        abuf[:, HALO_A - (taps_a - 1):HALO_A, :] = sta_ref[...]
        bbuf[:, HALO_B - (taps_b - 1):HALO_B, :] = stb_ref[...]
        qbuf[:, HALO_Q - (taps_q - 1):HALO_Q, :] = stq_ref[...]
        ns_ref[...] = sts_ref[...]

    n1g = n1g_ref[...]
    for s in range(sb):
        x = x_ref[s]
        m = mod_ref[s]
        h = (x * _rms_scale(x) * n1g) * (1.0 + m[:, d:2 * d]) + m[:, 0:d]
        h_scr[s * tt:(s + 1) * tt, :] = h.astype(BF16)
    hb = h_scr[...]
    z_a = _dot(hb, win_ref[:, 0:o_b])
    z_b = _dot(hb, win_ref[:, o_b:o_q])
    z_q = _dot(hb, win_ref[:, o_q:o_g])
    zg_scr[...] = _dot(hb, win_ref[:, o_g:o_s])
    z_s = _dot(hb, win_ref[:, o_s:o_s + LANES])
    ba_scr[...] = z_a[:, d_a:2 * d_a]
    u_a = z_a[:, 2 * d_a:3 * d_a] * z_a[:, 0:d_a]
    glu = z_b[:, 0:d_b] * _sigmoid(z_b[:, d_b:2 * d_b])
    for s in range(sb):
        abuf[s, HALO_A:HALO_A + tt, :] = u_a[s * tt:(s + 1) * tt]
        bbuf[s, HALO_B:HALO_B + tt, :] = glu[s * tt:(s + 1) * tt]
        qbuf[s, HALO_Q:HALO_Q + tt, :] = z_q[s * tt:(s + 1) * tt]
    lane = lax.broadcasted_iota(jnp.int32, z_s.shape, 1)
    g_all = -jnp.exp(alog_ref[...]) * _softplus(z_s + dtb_ref[...])
    gb_scr[...] = jnp.where(lane < heads, g_all, _sigmoid(z_s))

    ri = lax.broadcasted_iota(jnp.int32, (CHUNK, CHUNK), 0)
    ci = lax.broadcasted_iota(jnp.int32, (CHUNK, CHUNK), 1)
    incl = ri >= ci
    strict = ri > ci
    eye = jnp.where(ri == ci, 1.0, 0.0).astype(F32)
    masks = (eye,) + tuple((ri // b) == (ci // b) for b in (8, 16, 32))
    row_idx = lax.broadcasted_iota(jnp.int32, (CHUNK, LANES), 0)
    is_g_lane = lax.broadcasted_iota(jnp.int32, (CHUNK, LANES), 1) < heads
    cpt = tt // CHUNK
    caw = caw_ref[...]
    cbw = cbw_ref[...]
    cqw = cqw_ref[...]
    gng = gng_ref[...]

    grp = min(sb, STREAMS_PER_STEP)
    assert sb % grp == 0
    pairs = [(j, hd) for j in range(grp) for hd in range(heads)]

    def chunk_body(i, carry):
        if sb == grp:
            s0, c = 0, i
        elif cpt == 1:
            s0, c = i * grp, 0
        else:
            s0, c = (i // cpt) * grp, i % cpt
        r0 = pl.multiple_of(c * CHUNK, CHUNK)
        streams = [s0 + j for j in range(grp)]
        rows = [pl.multiple_of(s * tt + c * CHUNK, CHUNK) for s in streams]

        s_old = [ns_ref[streams[j], hd] for j, hd in pairs]

        y_a, y_b, qkv, gb, zg = [], [], [], [], []
        for s, row in zip(streams, rows):
            conv = _causal_taps(abuf[s, pl.ds(r0, CHUNK + HALO_A), :], caw, HALO_A)
            y_a.append(ba_scr[pl.ds(row, CHUNK), :] * conv)
            conv = _causal_taps(bbuf[s, pl.ds(r0, CHUNK + HALO_B), :], cbw, HALO_B) + cbb_ref[...]
            xc = conv - jnp.mean(conv, axis=-1, keepdims=True)
            ln = xc * lax.rsqrt(jnp.mean(xc * xc, axis=-1, keepdims=True) + EPS) * lng_ref[...] + lnb_ref[...]
            y_b.append(_silu(ln))
            qkv.append(_silu(_causal_taps(qbuf[s, pl.ds(r0, CHUNK + HALO_Q), :], cqw, HALO_Q)))
            gb.append(gb_scr[pl.ds(row, CHUNK), :])
            zg.append(zg_scr[pl.ds(row, CHUNK), :])
        gam = [_cumsum_rows(jnp.where(is_g_lane, g, 0.0), row_idx) for g in gb]
        gam_t = [g.T for g in gam]

        q = [qkv[j][:, hd * dk:(hd + 1) * dk] for j, hd in pairs]
        k = [qkv[j][:, d_c + hd * dk:d_c + (hd + 1) * dk] for j, hd in pairs]
        v = [qkv[j][:, 2 * d_c + hd * dk:2 * d_c + (hd + 1) * dk] for j, hd in pairs]
        q = [m * (lax.rsqrt(jnp.sum(m * m, axis=-1, keepdims=True) + EPS) * (dk ** -0.5)) for m in q]
        k = [m * lax.rsqrt(jnp.sum(m * m, axis=-1, keepdims=True) + EPS) for m in k]
        g_col = [gam[j][:, hd:hd + 1] for j, hd in pairs]
        g_row = [gam_t[j][hd:hd + 1, :] for j, hd in pairs]
        g_last = [gam[j][CHUNK - 1:CHUNK, hd:hd + 1] for j, hd in pairs]
        beta = [gb[j][:, heads + hd:heads + hd + 1] for j, hd in pairs]
        decay = [jnp.where(incl, jnp.exp(jnp.minimum(gc - gr, 0.0)), 0.0) for gc, gr in zip(g_col, g_row)]
        eg = [jnp.exp(gc) for gc in g_col]
        kb = [m.astype(BF16) for m in k]
        qb = [m.astype(BF16) for m in q]
        kk = [_dot(m, m, NT) for m in kb]
        qk = [_dot(a, b, NT) for a, b in zip(qb, kb)]
        a_low = [jnp.where(strict, b * m * dc, 0.0) for b, m, dc in zip(beta, kk, decay)]
        t_inv = _tri_inverse(a_low, masks)
        rhs = [jnp.concatenate([vv * b, kx * (b * e)], axis=-1).astype(BF16)
               for vv, kx, b, e in zip(v, k, beta, eg)]
        sol = [_dot(ti.astype(BF16), r) for ti, r in zip(t_inv, rhs)]
        sb16 = [m.astype(BF16) for m in s_old]
        w_s = [_dot(m[:, dk:2 * dk].astype(BF16), st) for m, st in zip(sol, sb16)]
        q_s = [_dot((m * e).astype(BF16), st) for m, e, st in zip(q, eg, sb16)]
        vb = [(m[:, 0:dk] - ws).astype(BF16) for m, ws in zip(sol, w_s)]
        o_in = [_dot((m * dc).astype(BF16), vn) for m, dc, vn in zip(qk, decay, vb)]
        k_v = [_dot((kx * jnp.exp(gl - gc)).astype(BF16), vn, TN) for kx, gl, gc, vn in zip(k, g_last, g_col, vb)]
        s_new = [st * jnp.exp(gl) + kv for st, gl, kv in zip(s_old, g_last, k_v)]
        o = [a + b for a, b in zip(q_s, o_in)]
        y_c = [m * _rms_scale(m) * gng * _silu(zg[j][:, hd * dk:(hd + 1) * dk]) for m, (j, hd) in zip(o, pairs)]

        for (j, hd), m in zip(pairs, s_new):
            ns_ref[streams[j], hd] = m
        for j, row in enumerate(rows):
            parts = [y_a[j], y_b[j]] + y_c[j * heads:(j + 1) * heads]
            ymix_scr[pl.ds(row, CHUNK), :] = jnp.concatenate(parts, axis=-1).astype(BF16)
        return carry

    lax.fori_loop(0, (sb // grp) * cpt, chunk_body, 0)

    y = _dot(ymix_scr[...], wout_ref[...])
    for s in range(sb):
        g1 = mod_ref[s][:, 2 * d:3 * d]
        o_ref[s] = x_ref[s] + g1 * y[s * tt:(s + 1) * tt]

    @pl.when(t == n_t - 1)
    def _store_state():
        na_ref[...] = abuf[:, tt + HALO_A - (taps_a - 1):tt + HALO_A, :]
        nb_ref[...] = bbuf[:, tt + HALO_B - (taps_b - 1):tt + HALO_B, :]
        nq_ref[...] = qbuf[:, tt + HALO_Q - (taps_q - 1):tt + HALO_Q, :]

    abuf[:, 0:HALO_A, :] = abuf[:, tt:tt + HALO_A, :]
    bbuf[:, 0:HALO_B, :] = bbuf[:, tt:tt + HALO_B, :]
    qbuf[:, 0:HALO_Q, :] = qbuf[:, tt:tt + HALO_Q, :]


def _tiles(n_streams, t_len):
    grp = STREAMS_PER_STEP if n_streams % STREAMS_PER_STEP == 0 else 1
    tt = min(t_len, MAX_TILE_ROWS // grp)
    assert tt % CHUNK == 0 and t_len % tt == 0
    sb = max(grp, min(n_streams, MAX_TILE_ROWS // tt))
    while n_streams % sb or sb % grp:
        sb -= 1
    return sb, tt


def _full(shape):
    return pl.BlockSpec(shape, lambda b, t: (0,) * len(shape))


def _mixer_call(x, mod, p, st_a, st_b, st_q, st_s):
    n_b, t_len, d = x.shape
    d_a, d_b, d_c3 = p["conv_a_w"].shape[-1], p["conv_b_w"].shape[-1], p["conv_qkv_w"].shape[-1]
    d_c = d_c3 // 3
    heads = st_s.shape[1]
    taps_a, taps_b, taps_q = p["conv_a_w"].shape[0], p["conv_b_w"].shape[0], p["conv_qkv_w"].shape[0]
    assert taps_a - 1 <= HALO_A and taps_b - 1 <= HALO_B and taps_q - 1 <= HALO_Q
    assert d_c // heads == LANES and 2 * heads <= LANES
    sb, tt = _tiles(n_b, t_len)
    rows = sb * tt
    dims = (sb, tt, d, d_a, d_b, d_c, heads, taps_a, taps_b, taps_q)
    win = p["w_in_pad"]

    def per_stream(shape):
        return pl.BlockSpec((sb,) + shape, lambda b, t: (b,) + (0,) * len(shape))

    x_spec = pl.BlockSpec((sb, tt, d), lambda b, t: (b, t, 0))
    state_shapes = [st_a.shape[1:], st_b.shape[1:], st_q.shape[1:], st_s.shape[1:]]
    in_specs = [x_spec, per_stream((1, mod.shape[-1])), _full((1, d)), _full(win.shape),
                _full(p["conv_a_w"].shape), _full(p["conv_b_w"].shape), _full((1, d_b)), _full((1, d_b)),
                _full((1, d_b)), _full(p["conv_qkv_w"].shape), _full((1, LANES)), _full((1, LANES)),
                _full((1, LANES)), _full(p["w_out_bf"].shape)] + [per_stream(s) for s in state_shapes]
    out_specs = [x_spec] + [per_stream(s) for s in state_shapes]
    out_shape = [jax.ShapeDtypeStruct(x.shape, F32)] + [jax.ShapeDtypeStruct(a.shape, F32) for a in (st_a, st_b, st_q, st_s)]
    scratch = [
        pltpu.VMEM((rows, d), BF16),
        pltpu.VMEM((sb, tt + HALO_A, d_a), F32),
        pltpu.VMEM((rows, d_a), F32),
        pltpu.VMEM((sb, tt + HALO_B, d_b), F32),
        pltpu.VMEM((sb, tt + HALO_Q, d_c3), F32),
        pltpu.VMEM((rows, d_c), F32),
        pltpu.VMEM((rows, LANES), F32),
        pltpu.VMEM((rows, d), BF16),
    ]
    return pl.pallas_call(
        functools.partial(_mixer_kernel, dims),
        grid=(n_b // sb, t_len // tt),
        in_specs=in_specs,
        out_specs=out_specs,
        out_shape=out_shape,
        scratch_shapes=scratch,
        compiler_params=pltpu.CompilerParams(
            dimension_semantics=("arbitrary", "arbitrary"), vmem_limit_bytes=VMEM_LIMIT_BYTES),
        name="mixer",
    )(x, mod, p["norm1_g"], win, p["conv_a_w"], p["conv_b_w"], p["conv_b_b"], p["ln_b_g"], p["ln_b_b"],
      p["conv_qkv_w"], p["a_log_pad"], p["dt_bias_pad"], p["gdn_norm_g"], p["w_out_bf"], st_a, st_b, st_q, st_s)


def _ffn_kernel(dims, x_ref, mod_ref, n2g_ref, wgu_ref, wd_ref, fg_ref, o_ref, h_scr):
    sb, tt, d, ff, ff_chunks, final_norm = dims
    n2g = n2g_ref[...]
    for s in range(sb):
        x = x_ref[s]
        m = mod_ref[s]
        h = (x * _rms_scale(x) * n2g) * (1.0 + m[:, 4 * d:5 * d]) + m[:, 3 * d:4 * d]
        h_scr[s * tt:(s + 1) * tt, :] = h.astype(BF16)
    hb = h_scr[...]
    acc = None
    for c0, c1 in ff_chunks:
        gate = _dot(hb, wgu_ref[:, c0:c1])
        up = _dot(hb, wgu_ref[:, ff + c0:ff + c1])
        part = _dot((_silu(gate) * up).astype(BF16), wd_ref[c0:c1, :])
        acc = part if acc is None else acc + part
    for s in range(sb):
        out = x_ref[s] + mod_ref[s][:, 5 * d:6 * d] * acc[s * tt:(s + 1) * tt]
        if final_norm:
            out = out * _rms_scale(out) * fg_ref[...]
        o_ref[s] = out


def _ffn_call(x, mod, p, final_g, final_norm):
    n_b, t_len, d = x.shape
    ff = p["w_down_bf"].shape[0]
    sb, tt = _tiles(n_b, t_len)
    n_chunks = -(-ff // 1024)
    step = -(-ff // (n_chunks * 2 * LANES)) * 2 * LANES
    ff_chunks = tuple((c0, min(c0 + step, ff)) for c0 in range(0, ff, step))
    dims = (sb, tt, d, ff, ff_chunks, final_norm)
    x_spec = pl.BlockSpec((sb, tt, d), lambda b, t: (b, t, 0))
    return pl.pallas_call(
        functools.partial(_ffn_kernel, dims),
        grid=(n_b // sb, t_len // tt),
        in_specs=[x_spec, pl.BlockSpec((sb, 1, mod.shape[-1]), lambda b, t: (b, 0, 0)), _full((1, d)),
                  _full(p["w_gate_up_bf"].shape), _full(p["w_down_bf"].shape), _full((1, d))],
        out_specs=x_spec,
        out_shape=jax.ShapeDtypeStruct(x.shape, F32),
        scratch_shapes=[pltpu.VMEM((sb * tt, d), BF16)],
        compiler_params=pltpu.CompilerParams(
            dimension_semantics=("arbitrary", "arbitrary"), vmem_limit_bytes=VMEM_LIMIT_BYTES),
        name="ffn",
    )(x, mod, p["norm2_g"], p["w_gate_up_bf"], p["w_down_bf"], final_g)


def _layer_params(l, heads, norm1_g, w_in, conv_a_w, conv_b_w, conv_b_b, ln_b_g, ln_b_b, conv_qkv_w, a_log,
                  dt_bias, gdn_norm_g, w_out, norm2_g, w_gate_up, w_down):
    d, d_in = w_in.shape[1:]
    pad_small = lambda v: jnp.pad(v[l], (0, LANES - heads)).reshape(1, LANES)
    return {
        "norm1_g": norm1_g[l].reshape(1, d),
        "w_in_pad": jnp.pad(w_in[l], ((0, 0), (0, LANES - 2 * heads))).astype(BF16),
        "conv_a_w": conv_a_w[l], "conv_b_w": conv_b_w[l], "conv_b_b": conv_b_b[l].reshape(1, -1),
        "ln_b_g": ln_b_g[l].reshape(1, -1), "ln_b_b": ln_b_b[l].reshape(1, -1), "conv_qkv_w": conv_qkv_w[l],
        "a_log_pad": pad_small(a_log), "dt_bias_pad": pad_small(dt_bias),
        "gdn_norm_g": gdn_norm_g[l].reshape(1, -1), "w_out_bf": w_out[l].astype(BF16),
        "norm2_g": norm2_g[l].reshape(1, d), "w_gate_up_bf": w_gate_up[l].astype(BF16),
        "w_down_bf": w_down[l].astype(BF16),
    }


def _trunk(x, mods, layers, states, final_g):
    depth = len(layers)
    new = [[], [], [], []]
    for l, p in enumerate(layers):
        mod = mods[l][:, None, :]
        outs = _mixer_call(x, mod, p, *(st[l] for st in states))
        x = _ffn_call(outs[0], mod, p, final_g, l == depth - 1)
        for acc, o in zip(new, outs[1:]):
            acc.append(o)
    return (x,) + tuple(jnp.stack(n) for n in new)


def kernel(x_prompt, x_sample, state_conv_a, state_conv_b, state_conv_qkv, state_gdn, c_prompt, c_sample, norm1_g, w_ada, b_ada, w_in, conv_a_w, conv_b_w, conv_b_b, ln_b_g, ln_b_b, conv_qkv_w, a_log, dt_bias, gdn_norm_g, w_out, norm2_g, w_gate_up, w_down, final_norm_g):
    depth = w_in.shape[0]
    heads = a_log.shape[-1]
    bp = x_prompt.shape[0]
    layers = [_layer_params(l, heads, norm1_g, w_in, conv_a_w, conv_b_w, conv_b_b, ln_b_g, ln_b_b, conv_qkv_w,
                            a_log, dt_bias, gdn_norm_g, w_out, norm2_g, w_gate_up, w_down) for l in range(depth)]
    mods = _ada_call(jnp.concatenate([c_prompt, c_sample], axis=0), w_ada, b_ada)
    final_g = final_norm_g.reshape(1, -1)
    zero_states = tuple(jnp.zeros((depth, bp) + st.shape[2:], F32)
                        for st in (state_conv_a, state_conv_b, state_conv_qkv, state_gdn))
    y_p, pa, pb, pq, ps = _trunk(x_prompt, mods[:, :bp], layers, zero_states, final_g)
    y_s, sa, sb, sq, ss = _trunk(x_sample, mods[:, bp:], layers,
                                 (state_conv_a, state_conv_b, state_conv_qkv, state_gdn), final_g)
    return (y_p, y_s, pa, pb, pq, ps, sa, sb, sq, ss)
```

```python
import functools

import jax
import jax.numpy as jnp
from jax import lax
from jax.experimental import pallas as pl
from jax.experimental.pallas import tpu as pltpu

F32 = jnp.float32
BF16 = jnp.bfloat16
EPS = 1e-6
CHUNK = 64
LANES = 128
SUBLANES = 8
HALO_A, HALO_B, HALO_Q = 8, 32, 8
VMEM_LIMIT_BYTES = 56 * 1024 * 1024
MAX_TILE_ROWS = 512
STREAMS_PER_STEP = 2

NN = ((1,), (0,))
NT = ((1,), (1,))
TN = ((0,), (0,))


def _sigmoid(x):
    return 0.5 * jnp.tanh(0.5 * x) + 0.5


def _silu(x):
    return x * _sigmoid(x)


def _softplus(x):
    return jnp.maximum(x, 0.0) + jnp.log1p(jnp.exp(-jnp.abs(x)))


def _dot(a, b, dims=NN):
    return lax.dot_general(a, b, (dims, ((), ())), preferred_element_type=F32)


def _bdot(a, b, dims=NN):
    return _dot(a.astype(BF16), b.astype(BF16), dims)


def _causal_taps(win, w, halo, residues=range(SUBLANES)):
    taps, n = w.shape[0], win.shape[0]
    first = halo - taps + 1
    acc = None
    for r in residues:
        offs = [o for o in range(first, halo + 1) if o % SUBLANES == r]
        if not offs:
            continue
        wr = win if r == 0 else pltpu.roll(win, n - r, axis=0)
        for o in offs:
            term = w[o - first:o - first + 1] * wr[o - r:o - r + CHUNK]
            acc = term if acc is None else acc + term
    return acc


def _rms_scale(x):
    return lax.rsqrt(jnp.mean(x * x, axis=-1, keepdims=True) + EPS)


def _cumsum_rows(x, row_idx):
    k = 1
    while k < x.shape[0]:
        x = x + jnp.where(row_idx >= k, pltpu.roll(x, k, axis=0), 0.0)
        k *= 2
    return x


def _ada_kernel(c_ref, w_ref, b_ref, o_ref):
    cm = _silu(c_ref[...])
    o_ref[0] = _dot(cm.astype(BF16), w_ref[0].astype(BF16)) + b_ref[0]


def _ada_call(c_all, w_ada, b_ada):
    depth, d, n6 = w_ada.shape
    rows = c_all.shape[0]
    tn = n6 // 4
    return pl.pallas_call(
        _ada_kernel,
        grid=(depth, n6 // tn),
        in_specs=[
            pl.BlockSpec((rows, d), lambda l, j: (0, 0)),
            pl.BlockSpec((1, d, tn), lambda l, j: (l, 0, j)),
            pl.BlockSpec((1, 1, tn), lambda l, j: (l, 0, j)),
        ],
        out_specs=pl.BlockSpec((1, rows, tn), lambda l, j: (l, 0, j)),
        out_shape=jax.ShapeDtypeStruct((depth, rows, n6), F32),
        compiler_params=pltpu.CompilerParams(
            dimension_semantics=("arbitrary", "arbitrary"), vmem_limit_bytes=VMEM_LIMIT_BYTES),
        name="ada_mod",
    )(c_all, w_ada, b_ada.reshape(depth, 1, n6))


def _rows(start, n):
    return pl.ds(start if isinstance(start, int) else pl.multiple_of(start, SUBLANES), n)


def _tri_inverse(mats, masks, tick):
    eye, m8, m16, m32 = masks
    a8 = [jnp.where(m8, a, 0.0) for a in mats]
    a2b = [_bdot(m, m).astype(BF16) for m in a8]
    tick()
    a4 = [_dot(b, b).astype(BF16) for b in a2b]
    p = [eye - m for m in a8]
    p = [m + _dot(m.astype(BF16), b) for m, b in zip(p, a2b)]
    tick()
    x = [m + _dot(m.astype(BF16), b) for m, b in zip(p, a4)]
    tick()
    prev = m8
    for m in (m16, m32, None):
        sel = jnp.logical_not(prev) if m is None else jnp.logical_and(m, jnp.logical_not(prev))
        xb = [v.astype(BF16) for v in x]
        ex = [_dot(jnp.where(sel, a, 0.0).astype(BF16), b).astype(BF16) for a, b in zip(mats, xb)]
        tick()
        x = [v - _dot(b, e) for v, b, e in zip(x, xb, ex)]
        tick()
        prev = m
    return x


def _mixer_kernel(dims, x_ref, xn_ref, mod_ref, n1g_ref, win_ref, caw_ref, cbw_ref, cbb_ref, lng_ref, lnb_ref,
                  cqw_ref, alog_ref, dtb_ref, gng_ref, wout_ref, sta_ref, stb_ref, stq_ref, sts_ref,
                  o_ref, na_ref, nb_ref, nq_ref, ns_ref,
                  abuf, ba_scr, bbuf, qbuf, zg_scr, gb_scr, f_qkv, f_yab, f_gam):
    sb, tt, d, d_a, d_b, d_c, heads, taps_a, taps_b, taps_q = dims
    dk = d_c // heads
    o_b = 3 * d_a
    o_q = o_b + 2 * d_b
    o_g = o_q + 3 * d_c
    o_s = o_g + d_c
    cpt = tt // CHUNK
    grp = min(sb, STREAMS_PER_STEP)
    assert sb % grp == 0
    n_sg = sb // grp
    pairs = [(j, hd) for j in range(grp) for hd in range(heads)]
    t = pl.program_id(1)
    n_t = pl.num_programs(1)
    par = lax.rem(t, 2)

    n1g = n1g_ref[...]
    caw = caw_ref[...]
    cbw = cbw_ref[...]
    cqw = cqw_ref[...]
    gng = gng_ref[...]
    ri = lax.broadcasted_iota(jnp.int32, (CHUNK, CHUNK), 0)
    ci = lax.broadcasted_iota(jnp.int32, (CHUNK, CHUNK), 1)
    incl = ri >= ci
    strict = ri > ci
    eye = jnp.where(ri == ci, 1.0, 0.0).astype(F32)
    masks = (eye,) + tuple((ri // b) == (ci // b) for b in (8, 16, 32))
    row_idx = lax.broadcasted_iota(jnp.int32, (CHUNK, LANES), 0)
    is_g_lane = lax.broadcasted_iota(jnp.int32, (CHUNK, LANES), 1) < heads

    def adaln(x, m):
        return ((x * _rms_scale(x) * n1g) * (1.0 + m[:, d:2 * d]) + m[:, 0:d]).astype(BF16)

    def proj_pieces(hb, dst, places, n):
        def put(ref, halo, vals, cols=None):
            for s, r0, off in places:
                if cols is None:
                    ref[dst, s, _rows(halo + r0, n), :] = vals[off:off + n]
                else:
                    ref[dst, s, _rows(halo + r0, n), cols[0]:cols[1]] = vals[off:off + n]

        tile = 2 * LANES

        def piece_a():
            h_a = _dot(hb, win_ref[:, 0:d_a])
            c_a = _dot(hb, win_ref[:, 2 * d_a:3 * d_a])
            put(abuf, HALO_A, c_a * h_a)

        def piece_ba():
            put(ba_scr, 0, _dot(hb, win_ref[:, d_a:2 * d_a]))

        def piece_b():
            z = _dot(hb, win_ref[:, o_b:o_q])
            put(bbuf, HALO_B, z[:, 0:d_b] * _sigmoid(z[:, d_b:2 * d_b]))

        def piece_cols(ref, halo, base, c0):
            def run():
                put(ref, halo, _dot(hb, win_ref[:, c0:c0 + tile]), (c0 - base, c0 - base + tile))
            return run

        def piece_s():
            z = _dot(hb, win_ref[:, o_s:o_s + LANES])
            lane = lax.broadcasted_iota(jnp.int32, z.shape, 1)
            g_all = -jnp.exp(alog_ref[...]) * _softplus(z + dtb_ref[...])
            put(gb_scr, 0, jnp.where(lane < heads, g_all, _sigmoid(z)))

        pieces = [piece_a, piece_ba, piece_b]
        pieces += [piece_cols(qbuf, HALO_Q, o_q, c0) for c0 in range(o_q, o_g, tile)]
        pieces += [piece_cols(zg_scr, 0, o_g, c0) for c0 in range(o_g, o_s, tile)]
        return pieces + [piece_s]

    def front_pieces(src, s, r0, slot):
        def load_b():
            return bbuf[src, s, _rows(r0, CHUNK + HALO_B), :]
        part = {}

        def conv_b_low():
            part["b"] = _causal_taps(load_b(), cbw, HALO_B, range(0, SUBLANES // 2))

        def conv_ab():
            conv = part["b"] + _causal_taps(load_b(), cbw, HALO_B, range(SUBLANES // 2, SUBLANES)) + cbb_ref[...]
            xc = conv - jnp.mean(conv, axis=-1, keepdims=True)
            ln = xc * lax.rsqrt(jnp.mean(xc * xc, axis=-1, keepdims=True) + EPS) * lng_ref[...] + lnb_ref[...]
            conv = _causal_taps(abuf[src, s, _rows(r0, CHUNK + HALO_A), :], caw, HALO_A)
            y_a = ba_scr[src, s, _rows(r0, CHUNK), :] * conv
            f_yab[s, slot] = jnp.concatenate([y_a, _silu(ln)], axis=-1).astype(BF16)

        def conv_qkv(j):
            def run():
                c0 = j * d_c
                win = qbuf[src, s, _rows(r0, CHUNK + HALO_Q), c0:c0 + d_c]
                act = _silu(_causal_taps(win, cqw[:, c0:c0 + d_c], HALO_Q))
                if j < 2:
                    scale = (dk ** -0.5) if j == 0 else 1.0
                    cols = [act[:, hd * dk:(hd + 1) * dk] for hd in range(heads)]
                    act = jnp.concatenate(
                        [m * (lax.rsqrt(jnp.sum(m * m, axis=-1, keepdims=True) + EPS) * scale) for m in cols], axis=-1)
                f_qkv[s, slot, :, c0:c0 + d_c] = act
            return run

        def decay_sum():
            gbv = gb_scr[src, s, _rows(r0, CHUNK), :]
            f_gam[s, slot] = _cumsum_rows(jnp.where(is_g_lane, gbv, 0.0), row_idx)

        return [conv_b_low, conv_ab, conv_qkv(0), conv_qkv(1), conv_qkv(2), decay_sum]

    def back_load(streams, r0, slot):
        return dict(
            s_old=[ns_ref[streams[j], hd] for j, hd in pairs],
            qkv=[f_qkv[s, slot] for s in streams],
            gam=[f_gam[s, slot] for s in streams],
            yab=[f_yab[s, slot] for s in streams],
            gb=[gb_scr[par, s, _rows(r0, CHUNK), :] for s in streams],
            zg=[zg_scr[par, s, _rows(r0, CHUNK), :] for s in streams],
        )

    def back_compute(v_in, tick):
        s_old, qkv, gam, gb, zg = v_in["s_old"], v_in["qkv"], v_in["gam"], v_in["gb"], v_in["zg"]
        gam_t = [g.T for g in gam]
        q = [qkv[j][:, hd * dk:(hd + 1) * dk] for j, hd in pairs]
        k = [qkv[j][:, d_c + hd * dk:d_c + (hd + 1) * dk] for j, hd in pairs]
        v = [qkv[j][:, 2 * d_c + hd * dk:2 * d_c + (hd + 1) * dk] for j, hd in pairs]
        g_col = [gam[j][:, hd:hd + 1] for j, hd in pairs]
        g_row = [gam_t[j][hd:hd + 1, :] for j, hd in pairs]
        g_last = [gam[j][CHUNK - 1:CHUNK, hd:hd + 1] for j, hd in pairs]
        beta = [gb[j][:, heads + hd:heads + hd + 1] for j, hd in pairs]
        decay = [jnp.where(incl, jnp.exp(jnp.minimum(gc - gr, 0.0)), 0.0) for gc, gr in zip(g_col, g_row)]
        eg = [jnp.exp(gc) for gc in g_col]
        kb = [m.astype(BF16) for m in k]
        qb = [m.astype(BF16) for m in q]
        kk = [_dot(m, m, NT) for m in kb]
        qk = [_dot(a, b, NT) for a, b in zip(qb, kb)]
        tick()
        a_low = [jnp.where(strict, b * m * dc, 0.0) for b, m, dc in zip(beta, kk, decay)]
        t_inv = _tri_inverse(a_low, masks, tick)
        rhs = [jnp.concatenate([vv * b, kx * (b * e)], axis=-1).astype(BF16)
               for vv, kx, b, e in zip(v, k, beta, eg)]
        sol = [_dot(ti.astype(BF16), r) for ti, r in zip(t_inv, rhs)]
        tick()
        sb16 = [m.astype(BF16) for m in s_old]
        w_s = [_dot(m[:, dk:2 * dk].astype(BF16), st) for m, st in zip(sol, sb16)]
        q_s = [_dot((m * e).astype(BF16), st) for m, e, st in zip(q, eg, sb16)]
        tick()
        vb = [(m[:, 0:dk] - ws).astype(BF16) for m, ws in zip(sol, w_s)]
        o_in = [_dot((m * dc).astype(BF16), vn) for m, dc, vn in zip(qk, decay, vb)]
        k_v = [_dot((kx * jnp.exp(gl - gc)).astype(BF16), vn, TN) for kx, gl, gc, vn in zip(k, g_last, g_col, vb)]
        tick()
        s_new = [st * jnp.exp(gl) + kv for st, gl, kv in zip(s_old, g_last, k_v)]
        o = [a + b for a, b in zip(q_s, o_in)]
        y_c = [m * _rms_scale(m) * gng * _silu(zg[j][:, hd * dk:(hd + 1) * dk]) for m, (j, hd) in zip(o, pairs)]
        return s_new, y_c

    @pl.when(t == 0)
    def _first_tile():
        abuf[0, :, HALO_A - (taps_a - 1):HALO_A, :] = sta_ref[...]
        bbuf[0, :, HALO_B - (taps_b - 1):HALO_B, :] = stb_ref[...]
        qbuf[0, :, HALO_Q - (taps_q - 1):HALO_Q, :] = stq_ref[...]
        ns_ref[...] = sts_ref[...]
        hb = jnp.concatenate([adaln(x_ref[s], mod_ref[s]) for s in range(sb)], axis=0)
        for piece in proj_pieces(hb, 0, [(s, 0, s * tt) for s in range(sb)], tt):
            piece()

        def first_front(s, carry):
            for piece in front_pieces(0, s, 0, 0):
                piece()
            return carry
        lax.fori_loop(0, sb, first_front, 0)

    abuf[1 - par, :, 0:HALO_A, :] = abuf[par, :, tt:tt + HALO_A, :]
    bbuf[1 - par, :, 0:HALO_B, :] = bbuf[par, :, tt:tt + HALO_B, :]
    qbuf[1 - par, :, 0:HALO_Q, :] = qbuf[par, :, tt:tt + HALO_Q, :]

    def chunk_step(i, carry):
        if n_sg == 1:
            sg, c = 0, i
        elif cpt == 1:
            sg, c = i, 0
        else:
            sg, c = i // cpt, lax.rem(i, cpt)
        streams = [sg * grp + j for j in range(grp)]
        r0 = c * CHUNK
        slot = lax.rem(t * cpt + c, 2)
        if cpt == 1:
            c_next, src_next = 0, 1 - par
        else:
            wrap = (c + 1) == cpt
            c_next, src_next = jnp.where(wrap, 0, c + 1), jnp.where(wrap, 1 - par, par)

        v_in = back_load(streams, r0, slot)
        hb = jnp.concatenate([adaln(xn_ref[s, _rows(r0, CHUNK), :], mod_ref[s]) for s in streams], axis=0)
        pieces = proj_pieces(hb, 1 - par, [(s, r0, j * CHUNK) for j, s in enumerate(streams)], CHUNK)
        fronts = [piece for s in streams for piece in front_pieces(src_next, s, c_next * CHUNK, 1 - slot)]
        if cpt == 1:
            for piece in pieces:
                piece()
            pieces = []
        todo_front, todo_proj = iter(fronts), iter(pieces)

        def tick():
            for todo in (todo_front, todo_proj):
                piece = next(todo, None)
                if piece is not None:
                    piece()
        s_new, y_c = back_compute(v_in, tick)
        for piece in todo_proj:
            piece()
        for piece in todo_front:
            piece()

        for (j, hd), m in zip(pairs, s_new):
            ns_ref[streams[j], hd] = m
        ymix = jnp.concatenate(
            [jnp.concatenate([v_in["yab"][j]] + [m.astype(BF16) for m in y_c[j * heads:(j + 1) * heads]], axis=-1)
             for j in range(grp)], axis=0)
        y = _dot(ymix, wout_ref[...])
        for j, s in enumerate(streams):
            g1 = mod_ref[s][:, 2 * d:3 * d]
            o_ref[s, _rows(r0, CHUNK), :] = x_ref[s, _rows(r0, CHUNK), :] + g1 * y[j * CHUNK:(j + 1) * CHUNK]
        return carry

    lax.fori_loop(0, n_sg * cpt, chunk_step, 0)

    @pl.when(t == n_t - 1)
    def _store_state():
        na_ref[...] = abuf[par, :, tt + HALO_A - (taps_a - 1):tt + HALO_A, :]
        nb_ref[...] = bbuf[par, :, tt + HALO_B - (taps_b - 1):tt + HALO_B, :]
        nq_ref[...] = qbuf[par, :, tt + HALO_Q - (taps_q - 1):tt + HALO_Q, :]


def _tiles(n_streams, t_len):
    grp = STREAMS_PER_STEP if n_streams % STREAMS_PER_STEP == 0 else 1
    tt = min(t_len, MAX_TILE_ROWS // grp)
    assert tt % CHUNK == 0 and t_len % tt == 0
    sb = max(grp, min(n_streams, MAX_TILE_ROWS // tt))
    while n_streams % sb or sb % grp:
        sb -= 1
    return sb, tt


def _full(shape, single_buffer=False):
    if single_buffer:
        return pl.BlockSpec(shape, lambda b, t: (0,) * len(shape), pipeline_mode=pl.Buffered(1))
    return pl.BlockSpec(shape, lambda b, t: (0,) * len(shape))


def _mixer_call(x, mod, p, st_a, st_b, st_q, st_s):
    n_b, t_len, d = x.shape
    d_a, d_b, d_c3 = p["conv_a_w"].shape[-1], p["conv_b_w"].shape[-1], p["conv_qkv_w"].shape[-1]
    d_c = d_c3 // 3
    heads = st_s.shape[1]
    taps_a, taps_b, taps_q = p["conv_a_w"].shape[0], p["conv_b_w"].shape[0], p["conv_qkv_w"].shape[0]
    assert taps_a - 1 <= HALO_A and taps_b - 1 <= HALO_B and taps_q - 1 <= HALO_Q
    assert d_c // heads == LANES and 2 * heads <= LANES
    sb, tt = _tiles(n_b, t_len)
    n_t = t_len // tt
    dims = (sb, tt, d, d_a, d_b, d_c, heads, taps_a, taps_b, taps_q)
    win = p["w_in_pad"]

    def per_stream(shape):
        return pl.BlockSpec((sb,) + shape, lambda b, t: (b,) + (0,) * len(shape))

    x_spec = pl.BlockSpec((sb, tt, d), lambda b, t: (b, t, 0))
    x_next_spec = pl.BlockSpec((sb, tt, d), lambda b, t: (b, jnp.minimum(t + 1, n_t - 1), 0))
    state_shapes = [st_a.shape[1:], st_b.shape[1:], st_q.shape[1:], st_s.shape[1:]]
    in_specs = [x_spec, x_next_spec, per_stream((1, mod.shape[-1])), _full((1, d)), _full(win.shape, True),
                _full(p["conv_a_w"].shape), _full(p["conv_b_w"].shape), _full((1, d_b)), _full((1, d_b)),
                _full((1, d_b)), _full(p["conv_qkv_w"].shape), _full((1, LANES)), _full((1, LANES)),
                _full((1, LANES)), _full(p["w_out_bf"].shape, True)] + [per_stream(s) for s in state_shapes]
    out_specs = [x_spec] + [per_stream(s) for s in state_shapes]
    out_shape = [jax.ShapeDtypeStruct(x.shape, F32)] + [jax.ShapeDtypeStruct(a.shape, F32) for a in (st_a, st_b, st_q, st_s)]
    scratch = [
        pltpu.VMEM((2, sb, tt + HALO_A, d_a), F32),
        pltpu.VMEM((2, sb, tt, d_a), F32),
        pltpu.VMEM((2, sb, tt + HALO_B, d_b), F32),
        pltpu.VMEM((2, sb, tt + HALO_Q, d_c3), F32),
        pltpu.VMEM((2, sb, tt, d_c), F32),
        pltpu.VMEM((2, sb, tt, LANES), F32),
        pltpu.VMEM((sb, 2, CHUNK, d_c3), F32),
        pltpu.VMEM((sb, 2, CHUNK, d_a + d_b), BF16),
        pltpu.VMEM((sb, 2, CHUNK, LANES), F32),
    ]
    return pl.pallas_call(
        functools.partial(_mixer_kernel, dims),
        grid=(n_b // sb, n_t),
        in_specs=in_specs,
        out_specs=out_specs,
        out_shape=out_shape,
        scratch_shapes=scratch,
        compiler_params=pltpu.CompilerParams(
            dimension_semantics=("arbitrary", "arbitrary"), vmem_limit_bytes=VMEM_LIMIT_BYTES),
        name="mixer",
    )(x, x, mod, p["norm1_g"], win, p["conv_a_w"], p["conv_b_w"], p["conv_b_b"], p["ln_b_g"], p["ln_b_b"],
      p["conv_qkv_w"], p["a_log_pad"], p["dt_bias_pad"], p["gdn_norm_g"], p["w_out_bf"], st_a, st_b, st_q, st_s)


def _ffn_kernel(dims, x_ref, mod_ref, n2g_ref, wgu_ref, wd_ref, fg_ref, o_ref, h_scr):
    sb, tt, d, ff, ff_chunks, final_norm = dims
    n2g = n2g_ref[...]
    for s in range(sb):
        x = x_ref[s]
        m = mod_ref[s]
        h = (x * _rms_scale(x) * n2g) * (1.0 + m[:, 4 * d:5 * d]) + m[:, 3 * d:4 * d]
        h_scr[s * tt:(s + 1) * tt, :] = h.astype(BF16)
    hb = h_scr[...]
    acc = None
    for c0, c1 in ff_chunks:
        gate = _dot(hb, wgu_ref[:, c0:c1])
        up = _dot(hb, wgu_ref[:, ff + c0:ff + c1])
        part = _dot((_silu(gate) * up).astype(BF16), wd_ref[c0:c1, :])
        acc = part if acc is None else acc + part
    for s in range(sb):
        out = x_ref[s] + mod_ref[s][:, 5 * d:6 * d] * acc[s * tt:(s + 1) * tt]
        if final_norm:
            out = out * _rms_scale(out) * fg_ref[...]
        o_ref[s] = out


def _ffn_call(x, mod, p, final_g, final_norm):
    n_b, t_len, d = x.shape
    ff = p["w_down_bf"].shape[0]
    sb, tt = _tiles(n_b, t_len)
    n_chunks = -(-ff // 1024)
    step = -(-ff // (n_chunks * 2 * LANES)) * 2 * LANES
    ff_chunks = tuple((c0, min(c0 + step, ff)) for c0 in range(0, ff, step))
    dims = (sb, tt, d, ff, ff_chunks, final_norm)
    x_spec = pl.BlockSpec((sb, tt, d), lambda b, t: (b, t, 0))
    return pl.pallas_call(
        functools.partial(_ffn_kernel, dims),
        grid=(n_b // sb, t_len // tt),
        in_specs=[x_spec, pl.BlockSpec((sb, 1, mod.shape[-1]), lambda b, t: (b, 0, 0)), _full((1, d)),
                  _full(p["w_gate_up_bf"].shape), _full(p["w_down_bf"].shape), _full((1, d))],
        out_specs=x_spec,
        out_shape=jax.ShapeDtypeStruct(x.shape, F32),
        scratch_shapes=[pltpu.VMEM((sb * tt, d), BF16)],
        compiler_params=pltpu.CompilerParams(
            dimension_semantics=("arbitrary", "arbitrary"), vmem_limit_bytes=VMEM_LIMIT_BYTES),
        name="ffn",
    )(x, mod, p["norm2_g"], p["w_gate_up_bf"], p["w_down_bf"], final_g)


def _layer_params(l, heads, norm1_g, w_in, conv_a_w, conv_b_w, conv_b_b, ln_b_g, ln_b_b, conv_qkv_w, a_log,
                  dt_bias, gdn_norm_g, w_out, norm2_g, w_gate_up, w_down):
    d, d_in = w_in.shape[1:]
    pad_small = lambda v: jnp.pad(v[l], (0, LANES - heads)).reshape(1, LANES)
    return {
        "norm1_g": norm1_g[l].reshape(1, d),
        "w_in_pad": jnp.pad(w_in[l], ((0, 0), (0, LANES - 2 * heads))).astype(BF16),
        "conv_a_w": conv_a_w[l], "conv_b_w": conv_b_w[l], "conv_b_b": conv_b_b[l].reshape(1, -1),
        "ln_b_g": ln_b_g[l].reshape(1, -1), "ln_b_b": ln_b_b[l].reshape(1, -1), "conv_qkv_w": conv_qkv_w[l],
        "a_log_pad": pad_small(a_log), "dt_bias_pad": pad_small(dt_bias),
        "gdn_norm_g": gdn_norm_g[l].reshape(1, -1), "w_out_bf": w_out[l].astype(BF16),
        "norm2_g": norm2_g[l].reshape(1, d), "w_gate_up_bf": w_gate_up[l].astype(BF16),
        "w_down_bf": w_down[l].astype(BF16),
    }


def _trunk(x, mods, layers, states, final_g):
    depth = len(layers)
    new = [[], [], [], []]
    for l, p in enumerate(layers):
        mod = mods[l][:, None, :]
        outs = _mixer_call(x, mod, p, *(st[l] for st in states))
        x = _ffn_call(outs[0], mod, p, final_g, l == depth - 1)
        for acc, o in zip(new, outs[1:]):
            acc.append(o)
    return (x,) + tuple(jnp.stack(n) for n in new)


def kernel(x_prompt, x_sample, state_conv_a, state_conv_b, state_conv_qkv, state_gdn, c_prompt, c_sample, norm1_g, w_ada, b_ada, w_in, conv_a_w, conv_b_w, conv_b_b, ln_b_g, ln_b_b, conv_qkv_w, a_log, dt_bias, gdn_norm_g, w_out, norm2_g, w_gate_up, w_down, final_norm_g):
    depth = w_in.shape[0]
    heads = a_log.shape[-1]
    bp = x_prompt.shape[0]
    layers = [_layer_params(l, heads, norm1_g, w_in, conv_a_w, conv_b_w, conv_b_b, ln_b_g, ln_b_b, conv_qkv_w,
                            a_log, dt_bias, gdn_norm_g, w_out, norm2_g, w_gate_up, w_down) for l in range(depth)]
    mods = _ada_call(jnp.concatenate([c_prompt, c_sample], axis=0), w_ada, b_ada)
    final_g = final_norm_g.reshape(1, -1)
    zero_states = tuple(jnp.zeros((depth, bp) + st.shape[2:], F32)
                        for st in (state_conv_a, state_conv_b, state_conv_qkv, state_gdn))
    y_p, pa, pb, pq, ps = _trunk(x_prompt, mods[:, :bp], layers, zero_states, final_g)
    y_s, sa, sb, sq, ss = _trunk(x_sample, mods[:, bp:], layers,
                                 (state_conv_a, state_conv_b, state_conv_qkv, state_gdn), final_g)
    return (y_p, y_s, pa, pb, pq, ps, sa, sb, sq, ss)
```

```python
import functools

import jax
import jax.numpy as jnp
from jax import lax
from jax.experimental import pallas as pl
from jax.experimental.pallas import tpu as pltpu

F32 = jnp.float32
BF16 = jnp.bfloat16
EPS = 1e-6
CHUNK = 64
LANES = 128
SUBLANES = 8
HALO_A, HALO_B, HALO_Q = 8, 32, 8
VMEM_LIMIT_BYTES = 56 * 1024 * 1024
MAX_TILE_ROWS = 512
STREAMS_PER_STEP = 4

NN = ((1,), (0,))
NT = ((1,), (1,))
TN = ((0,), (0,))


def _sigmoid(x):
    return 0.5 * jnp.tanh(0.5 * x) + 0.5


def _silu(x):
    return x * _sigmoid(x)


def _softplus(x):
    return jnp.maximum(x, 0.0) + jnp.log1p(jnp.exp(-jnp.abs(x)))


def _dot(a, b, dims=NN):
    return lax.dot_general(a, b, (dims, ((), ())), preferred_element_type=F32)


def _bdot(a, b, dims=NN):
    return _dot(a.astype(BF16), b.astype(BF16), dims)


def _causal_taps(win, w, halo, residues=range(SUBLANES)):
    taps, n = w.shape[0], win.shape[0]
    first = halo - taps + 1
    acc = None
    for r in residues:
        offs = [o for o in range(first, halo + 1) if o % SUBLANES == r]
        if not offs:
            continue
        wr = win if r == 0 else pltpu.roll(win, n - r, axis=0)
        for o in offs:
            term = w[o - first:o - first + 1] * wr[o - r:o - r + CHUNK]
            acc = term if acc is None else acc + term
    return acc


def _rms_scale(x):
    return lax.rsqrt(jnp.mean(x * x, axis=-1, keepdims=True) + EPS)


def _cumsum_rows(x, row_idx):
    k = 1
    while k < x.shape[0]:
        x = x + jnp.where(row_idx >= k, pltpu.roll(x, k, axis=0), 0.0)
        k *= 2
    return x


def _ada_kernel(c_ref, w_ref, b_ref, o_ref):
    cm = _silu(c_ref[...])
    o_ref[0] = _dot(cm.astype(BF16), w_ref[0].astype(BF16)) + b_ref[0]


def _ada_call(c_all, w_ada, b_ada):
    depth, d, n6 = w_ada.shape
    rows = c_all.shape[0]
    tn = n6 // 4
    return pl.pallas_call(
        _ada_kernel,
        grid=(depth, n6 // tn),
        in_specs=[
            pl.BlockSpec((rows, d), lambda l, j: (0, 0)),
            pl.BlockSpec((1, d, tn), lambda l, j: (l, 0, j)),
            pl.BlockSpec((1, 1, tn), lambda l, j: (l, 0, j)),
        ],
        out_specs=pl.BlockSpec((1, rows, tn), lambda l, j: (l, 0, j)),
        out_shape=jax.ShapeDtypeStruct((depth, rows, n6), F32),
        compiler_params=pltpu.CompilerParams(
            dimension_semantics=("arbitrary", "arbitrary"), vmem_limit_bytes=VMEM_LIMIT_BYTES),
        name="ada_mod",
    )(c_all, w_ada, b_ada.reshape(depth, 1, n6))


def _rows(start, n):
    return pl.ds(start if isinstance(start, int) else pl.multiple_of(start, SUBLANES), n)


def _tri_inverse(mats, masks, tick):
    eye, m8, m16, m32 = masks
    a8 = [jnp.where(m8, a, 0.0) for a in mats]
    a2b = [_bdot(m, m).astype(BF16) for m in a8]
    tick()
    a4 = [_dot(b, b).astype(BF16) for b in a2b]
    p = [eye - m for m in a8]
    p = [m + _dot(m.astype(BF16), b) for m, b in zip(p, a2b)]
    tick()
    x = [m + _dot(m.astype(BF16), b) for m, b in zip(p, a4)]
    tick()
    prev = m8
    for m in (m16, m32, None):
        sel = jnp.logical_not(prev) if m is None else jnp.logical_and(m, jnp.logical_not(prev))
        xb = [v.astype(BF16) for v in x]
        ex = [_dot(jnp.where(sel, a, 0.0).astype(BF16), b).astype(BF16) for a, b in zip(mats, xb)]
        tick()
        x = [v - _dot(b, e) for v, b, e in zip(x, xb, ex)]
        tick()
        prev = m
    return x


def _mixer_kernel(dims, x_ref, xn_ref, mod_ref, n1g_ref, win_ref, caw_ref, cbw_ref, cbb_ref, lng_ref, lnb_ref,
                  cqw_ref, alog_ref, dtb_ref, gng_ref, wout_ref, sta_ref, stb_ref, stq_ref, sts_ref,
                  o_ref, na_ref, nb_ref, nq_ref, ns_ref,
                  abuf, ba_scr, bbuf, qbuf, zg_scr, gb_scr, f_qkv, f_yab, f_gam):
    sb, tt, n_tiles, d, d_a, d_b, d_c, heads, taps_a, taps_b, taps_q = dims
    dk = d_c // heads
    o_b = 3 * d_a
    o_q = o_b + 2 * d_b
    o_g = o_q + 3 * d_c
    o_s = o_g + d_c
    cpt = tt // CHUNK
    grp = min(sb, STREAMS_PER_STEP)
    assert sb % grp == 0
    n_sg = sb // grp
    pairs = [(j, hd) for j in range(grp) for hd in range(heads)]
    t = pl.program_id(1)
    n_t = pl.num_programs(1)
    par = lax.rem(t, 2)

    n1g = n1g_ref[...]
    caw = caw_ref[...]
    cbw = cbw_ref[...]
    cqw = cqw_ref[...]
    gng = gng_ref[...]
    ri = lax.broadcasted_iota(jnp.int32, (CHUNK, CHUNK), 0)
    ci = lax.broadcasted_iota(jnp.int32, (CHUNK, CHUNK), 1)
    incl = ri >= ci
    strict = ri > ci
    eye = jnp.where(ri == ci, 1.0, 0.0).astype(F32)
    masks = (eye,) + tuple((ri // b) == (ci // b) for b in (8, 16, 32))
    row_idx = lax.broadcasted_iota(jnp.int32, (CHUNK, LANES), 0)
    is_g_lane = lax.broadcasted_iota(jnp.int32, (CHUNK, LANES), 1) < heads

    def adaln(x, m):
        return ((x * _rms_scale(x) * n1g) * (1.0 + m[:, d:2 * d]) + m[:, 0:d]).astype(BF16)

    def proj_pieces(hb, dst, places, n):
        def put(ref, halo, vals, cols=None):
            for s, r0, off in places:
                if cols is None:
                    ref[dst, s, _rows(halo + r0, n), :] = vals[off:off + n]
                else:
                    ref[dst, s, _rows(halo + r0, n), cols[0]:cols[1]] = vals[off:off + n]

        tile = 2 * LANES

        def piece_a():
            h_a = _dot(hb, win_ref[:, 0:d_a])
            c_a = _dot(hb, win_ref[:, 2 * d_a:3 * d_a])
            put(abuf, HALO_A, c_a * h_a)

        def piece_ba():
            put(ba_scr, 0, _dot(hb, win_ref[:, d_a:2 * d_a]))

        def piece_b():
            z = _dot(hb, win_ref[:, o_b:o_q])
            put(bbuf, HALO_B, z[:, 0:d_b] * _sigmoid(z[:, d_b:2 * d_b]))

        def piece_cols(ref, halo, base, c0):
            def run():
                put(ref, halo, _dot(hb, win_ref[:, c0:c0 + tile]), (c0 - base, c0 - base + tile))
            return run

        def piece_s():
            z = _dot(hb, win_ref[:, o_s:o_s + LANES])
            lane = lax.broadcasted_iota(jnp.int32, z.shape, 1)
            g_all = -jnp.exp(alog_ref[...]) * _softplus(z + dtb_ref[...])
            put(gb_scr, 0, jnp.where(lane < heads, g_all, _sigmoid(z)))

        pieces = [piece_a, piece_ba, piece_b]
        pieces += [piece_cols(qbuf, HALO_Q, o_q, c0) for c0 in range(o_q, o_g, tile)]
        pieces += [piece_cols(zg_scr, 0, o_g, c0) for c0 in range(o_g, o_s, tile)]
        return pieces + [piece_s]

    def front_pieces(src, s, r0, slot):
        def load_b():
            return bbuf[src, s, _rows(r0, CHUNK + HALO_B), :]
        part = {}

        def conv_b_low():
            part["b"] = _causal_taps(load_b(), cbw, HALO_B, range(0, SUBLANES // 2))

        def conv_ab():
            conv = part["b"] + _causal_taps(load_b(), cbw, HALO_B, range(SUBLANES // 2, SUBLANES)) + cbb_ref[...]
            xc = conv - jnp.mean(conv, axis=-1, keepdims=True)
            ln = xc * lax.rsqrt(jnp.mean(xc * xc, axis=-1, keepdims=True) + EPS) * lng_ref[...] + lnb_ref[...]
            conv = _causal_taps(abuf[src, s, _rows(r0, CHUNK + HALO_A), :], caw, HALO_A)
            y_a = ba_scr[src, s, _rows(r0, CHUNK), :] * conv
            f_yab[s, slot] = jnp.concatenate([y_a, _silu(ln)], axis=-1).astype(BF16)

        def conv_qkv(j):
            def run():
                c0 = j * d_c
                win = qbuf[src, s, _rows(r0, CHUNK + HALO_Q), c0:c0 + d_c]
                act = _silu(_causal_taps(win, cqw[:, c0:c0 + d_c], HALO_Q))
                if j < 2:
                    scale = (dk ** -0.5) if j == 0 else 1.0
                    cols = [act[:, hd * dk:(hd + 1) * dk] for hd in range(heads)]
                    act = jnp.concatenate(
                        [m * (lax.rsqrt(jnp.sum(m * m, axis=-1, keepdims=True) + EPS) * scale) for m in cols], axis=-1)
                f_qkv[s, slot, :, c0:c0 + d_c] = act
            return run

        def decay_sum():
            gbv = gb_scr[src, s, _rows(r0, CHUNK), :]
            f_gam[s, slot] = _cumsum_rows(jnp.where(is_g_lane, gbv, 0.0), row_idx)

        return [conv_b_low, conv_ab, conv_qkv(0), conv_qkv(1), conv_qkv(2), decay_sum]

    def back_load(streams, r0, slot):
        return dict(
            s_old=[ns_ref[streams[j], hd] for j, hd in pairs],
            qkv=[f_qkv[s, slot] for s in streams],
            gam=[f_gam[s, slot] for s in streams],
            yab=[f_yab[s, slot] for s in streams],
            gb=[gb_scr[par, s, _rows(r0, CHUNK), :] for s in streams],
            zg=[zg_scr[par, s, _rows(r0, CHUNK), :] for s in streams],
        )

    def back_compute(v_in, tick):
        s_old, qkv, gam, gb, zg = v_in["s_old"], v_in["qkv"], v_in["gam"], v_in["gb"], v_in["zg"]
        gam_t = [g.T for g in gam]
        q = [qkv[j][:, hd * dk:(hd + 1) * dk] for j, hd in pairs]
        k = [qkv[j][:, d_c + hd * dk:d_c + (hd + 1) * dk] for j, hd in pairs]
        v = [qkv[j][:, 2 * d_c + hd * dk:2 * d_c + (hd + 1) * dk] for j, hd in pairs]
        g_col = [gam[j][:, hd:hd + 1] for j, hd in pairs]
        g_row = [gam_t[j][hd:hd + 1, :] for j, hd in pairs]
        g_last = [gam[j][CHUNK - 1:CHUNK, hd:hd + 1] for j, hd in pairs]
        beta = [gb[j][:, heads + hd:heads + hd + 1] for j, hd in pairs]
        decay = [jnp.where(incl, jnp.exp(jnp.minimum(gc - gr, 0.0)), 0.0) for gc, gr in zip(g_col, g_row)]
        eg = [jnp.exp(gc) for gc in g_col]
        kb = [m.astype(BF16) for m in k]
        qb = [m.astype(BF16) for m in q]
        kk = [_dot(m, m, NT) for m in kb]
        qk = [_dot(a, b, NT) for a, b in zip(qb, kb)]
        tick()
        a_low = [jnp.where(strict, b * m * dc, 0.0) for b, m, dc in zip(beta, kk, decay)]
        t_inv = _tri_inverse(a_low, masks, tick)
        rhs = [jnp.concatenate([vv * b, kx * (b * e)], axis=-1).astype(BF16)
               for vv, kx, b, e in zip(v, k, beta, eg)]
        sol = [_dot(ti.astype(BF16), r) for ti, r in zip(t_inv, rhs)]
        tick()
        sb16 = [m.astype(BF16) for m in s_old]
        w_s = [_dot(m[:, dk:2 * dk].astype(BF16), st) for m, st in zip(sol, sb16)]
        q_s = [_dot((m * e).astype(BF16), st) for m, e, st in zip(q, eg, sb16)]
        tick()
        vb = [(m[:, 0:dk] - ws).astype(BF16) for m, ws in zip(sol, w_s)]
        o_in = [_dot((m * dc).astype(BF16), vn) for m, dc, vn in zip(qk, decay, vb)]
        k_v = [_dot((kx * jnp.exp(gl - gc)).astype(BF16), vn, TN) for kx, gl, gc, vn in zip(k, g_last, g_col, vb)]
        tick()
        s_new = [st * jnp.exp(gl) + kv for st, gl, kv in zip(s_old, g_last, k_v)]
        o = [a + b for a, b in zip(q_s, o_in)]
        y_c = [m * _rms_scale(m) * gng * _silu(zg[j][:, hd * dk:(hd + 1) * dk]) for m, (j, hd) in zip(o, pairs)]
        return s_new, y_c

    @pl.when(t == 0)
    def _first_tile():
        abuf[0, :, HALO_A - (taps_a - 1):HALO_A, :] = sta_ref[...]
        bbuf[0, :, HALO_B - (taps_b - 1):HALO_B, :] = stb_ref[...]
        qbuf[0, :, HALO_Q - (taps_q - 1):HALO_Q, :] = stq_ref[...]
        ns_ref[...] = sts_ref[...]
        hb = jnp.concatenate([adaln(x_ref[s], mod_ref[s]) for s in range(sb)], axis=0)
        for piece in proj_pieces(hb, 0, [(s, 0, s * tt) for s in range(sb)], tt):
            piece()

        def first_front(s, carry):
            for piece in front_pieces(0, s, 0, 0):
                piece()
            return carry
        lax.fori_loop(0, sb, first_front, 0)

    abuf[1 - par, :, 0:HALO_A, :] = abuf[par, :, tt:tt + HALO_A, :]
    bbuf[1 - par, :, 0:HALO_B, :] = bbuf[par, :, tt:tt + HALO_B, :]
    qbuf[1 - par, :, 0:HALO_Q, :] = qbuf[par, :, tt:tt + HALO_Q, :]

    def chunk_step(i, carry):
        if n_sg == 1:
            sg, c = 0, i
        elif cpt == 1:
            sg, c = i, 0
        else:
            sg, c = i // cpt, lax.rem(i, cpt)
        streams = [sg * grp + j for j in range(grp)]
        r0 = c * CHUNK
        slot = lax.rem(t * cpt + c, 2)
        if cpt == 1:
            c_next, src_next = 0, 1 - par
        else:
            wrap = (c + 1) == cpt
            c_next, src_next = jnp.where(wrap, 0, c + 1), jnp.where(wrap, 1 - par, par)

        v_in = back_load(streams, r0, slot)
        hb = jnp.concatenate([adaln(xn_ref[s, _rows(r0, CHUNK), :], mod_ref[s]) for s in streams], axis=0)
        pieces = proj_pieces(hb, 1 - par, [(s, r0, j * CHUNK) for j, s in enumerate(streams)], CHUNK)
        fronts = [piece for s in streams for piece in front_pieces(src_next, s, c_next * CHUNK, 1 - slot)]
        if n_tiles == 1:
            pieces = []
            if cpt == 1:
                fronts = []
        if cpt == 1:
            for piece in pieces:
                piece()
            pieces = []
        todo_front, todo_proj = iter(fronts), iter(pieces)

        def tick():
            for todo in (todo_front, todo_proj):
                piece = next(todo, None)
                if piece is not None:
                    piece()
        s_new, y_c = back_compute(v_in, tick)
        for piece in todo_proj:
            piece()
        for piece in todo_front:
            piece()

        for (j, hd), m in zip(pairs, s_new):
            ns_ref[streams[j], hd] = m
        ymix = jnp.concatenate(
            [jnp.concatenate([v_in["yab"][j]] + [m.astype(BF16) for m in y_c[j * heads:(j + 1) * heads]], axis=-1)
             for j in range(grp)], axis=0)
        y = _dot(ymix, wout_ref[...])
        for j, s in enumerate(streams):
            g1 = mod_ref[s][:, 2 * d:3 * d]
            o_ref[s, _rows(r0, CHUNK), :] = x_ref[s, _rows(r0, CHUNK), :] + g1 * y[j * CHUNK:(j + 1) * CHUNK]
        return carry

    lax.fori_loop(0, n_sg * cpt, chunk_step, 0)

    @pl.when(t == n_t - 1)
    def _store_state():
        na_ref[...] = abuf[par, :, tt + HALO_A - (taps_a - 1):tt + HALO_A, :]
        nb_ref[...] = bbuf[par, :, tt + HALO_B - (taps_b - 1):tt + HALO_B, :]
        nq_ref[...] = qbuf[par, :, tt + HALO_Q - (taps_q - 1):tt + HALO_Q, :]


def _tiles(n_streams, t_len):
    grp = STREAMS_PER_STEP
    while n_streams % grp or MAX_TILE_ROWS // grp < CHUNK:
        grp //= 2
    tt = min(t_len, MAX_TILE_ROWS // grp)
    assert tt % CHUNK == 0 and t_len % tt == 0
    sb = max(grp, min(n_streams, MAX_TILE_ROWS // tt))
    while n_streams % sb or sb % grp:
        sb -= 1
    return sb, tt


def _full(shape, single_buffer=False):
    if single_buffer:
        return pl.BlockSpec(shape, lambda b, t: (0,) * len(shape), pipeline_mode=pl.Buffered(1))
    return pl.BlockSpec(shape, lambda b, t: (0,) * len(shape))


def _mixer_call(x, mod, p, st_a, st_b, st_q, st_s):
    n_b, t_len, d = x.shape
    d_a, d_b, d_c3 = p["conv_a_w"].shape[-1], p["conv_b_w"].shape[-1], p["conv_qkv_w"].shape[-1]
    d_c = d_c3 // 3
    heads = st_s.shape[1]
    taps_a, taps_b, taps_q = p["conv_a_w"].shape[0], p["conv_b_w"].shape[0], p["conv_qkv_w"].shape[0]
    assert taps_a - 1 <= HALO_A and taps_b - 1 <= HALO_B and taps_q - 1 <= HALO_Q
    assert d_c // heads == LANES and 2 * heads <= LANES
    sb, tt = _tiles(n_b, t_len)
    n_t = t_len // tt
    dims = (sb, tt, n_t, d, d_a, d_b, d_c, heads, taps_a, taps_b, taps_q)
    win = p["w_in_pad"]

    def per_stream(shape):
        return pl.BlockSpec((sb,) + shape, lambda b, t: (b,) + (0,) * len(shape))

    x_spec = pl.BlockSpec((sb, tt, d), lambda b, t: (b, t, 0))
    x_next_spec = pl.BlockSpec((sb, tt, d), lambda b, t: (b, jnp.minimum(t + 1, n_t - 1), 0))
    state_shapes = [st_a.shape[1:], st_b.shape[1:], st_q.shape[1:], st_s.shape[1:]]
    in_specs = [x_spec, x_next_spec, per_stream((1, mod.shape[-1])), _full((1, d)), _full(win.shape, True),
                _full(p["conv_a_w"].shape), _full(p["conv_b_w"].shape), _full((1, d_b)), _full((1, d_b)),
                _full((1, d_b)), _full(p["conv_qkv_w"].shape), _full((1, LANES)), _full((1, LANES)),
                _full((1, LANES)), _full(p["w_out_bf"].shape, True)] + [per_stream(s) for s in state_shapes]
    out_specs = [x_spec] + [per_stream(s) for s in state_shapes]
    out_shape = [jax.ShapeDtypeStruct(x.shape, F32)] + [jax.ShapeDtypeStruct(a.shape, F32) for a in (st_a, st_b, st_q, st_s)]
    scratch = [
        pltpu.VMEM((2, sb, tt + HALO_A, d_a), F32),
        pltpu.VMEM((2, sb, tt, d_a), F32),
        pltpu.VMEM((2, sb, tt + HALO_B, d_b), F32),
        pltpu.VMEM((2, sb, tt + HALO_Q, d_c3), F32),
        pltpu.VMEM((2, sb, tt, d_c), F32),
        pltpu.VMEM((2, sb, tt, LANES), F32),
        pltpu.VMEM((sb, 2, CHUNK, d_c3), F32),
        pltpu.VMEM((sb, 2, CHUNK, d_a + d_b), BF16),
        pltpu.VMEM((sb, 2, CHUNK, LANES), F32),
    ]
    return pl.pallas_call(
        functools.partial(_mixer_kernel, dims),
        grid=(n_b // sb, n_t),
        in_specs=in_specs,
        out_specs=out_specs,
        out_shape=out_shape,
        scratch_shapes=scratch,
        compiler_params=pltpu.CompilerParams(
            dimension_semantics=("arbitrary", "arbitrary"), vmem_limit_bytes=VMEM_LIMIT_BYTES),
        name="mixer",
    )(x, x, mod, p["norm1_g"], win, p["conv_a_w"], p["conv_b_w"], p["conv_b_b"], p["ln_b_g"], p["ln_b_b"],
      p["conv_qkv_w"], p["a_log_pad"], p["dt_bias_pad"], p["gdn_norm_g"], p["w_out_bf"], st_a, st_b, st_q, st_s)


def _ffn_kernel(dims, x_ref, mod_ref, n2g_ref, wgu_ref, wd_ref, fg_ref, o_ref, h_scr):
    sb, tt, d, ff, ff_chunks, final_norm = dims
    n2g = n2g_ref[...]
    for s in range(sb):
        x = x_ref[s]
        m = mod_ref[s]
        h = (x * _rms_scale(x) * n2g) * (1.0 + m[:, 4 * d:5 * d]) + m[:, 3 * d:4 * d]
        h_scr[s * tt:(s + 1) * tt, :] = h.astype(BF16)
    hb = h_scr[...]
    acc = None
    for c0, c1 in ff_chunks:
        gate = _dot(hb, wgu_ref[:, c0:c1])
        up = _dot(hb, wgu_ref[:, ff + c0:ff + c1])
        part = _dot((_silu(gate) * up).astype(BF16), wd_ref[c0:c1, :])
        acc = part if acc is None else acc + part
    for s in range(sb):
        out = x_ref[s] + mod_ref[s][:, 5 * d:6 * d] * acc[s * tt:(s + 1) * tt]
        if final_norm:
            out = out * _rms_scale(out) * fg_ref[...]
        o_ref[s] = out


def _ffn_call(x, mod, p, final_g, final_norm):
    n_b, t_len, d = x.shape
    ff = p["w_down_bf"].shape[0]
    sb, tt = _tiles(n_b, t_len)
    n_chunks = -(-ff // 1024)
    step = -(-ff // (n_chunks * 2 * LANES)) * 2 * LANES
    ff_chunks = tuple((c0, min(c0 + step, ff)) for c0 in range(0, ff, step))
    dims = (sb, tt, d, ff, ff_chunks, final_norm)
    x_spec = pl.BlockSpec((sb, tt, d), lambda b, t: (b, t, 0))
    return pl.pallas_call(
        functools.partial(_ffn_kernel, dims),
        grid=(n_b // sb, t_len // tt),
        in_specs=[x_spec, pl.BlockSpec((sb, 1, mod.shape[-1]), lambda b, t: (b, 0, 0)), _full((1, d)),
                  _full(p["w_gate_up_bf"].shape), _full(p["w_down_bf"].shape), _full((1, d))],
        out_specs=x_spec,
        out_shape=jax.ShapeDtypeStruct(x.shape, F32),
        scratch_shapes=[pltpu.VMEM((sb * tt, d), BF16)],
        compiler_params=pltpu.CompilerParams(
            dimension_semantics=("arbitrary", "arbitrary"), vmem_limit_bytes=VMEM_LIMIT_BYTES),
        name="ffn",
    )(x, mod, p["norm2_g"], p["w_gate_up_bf"], p["w_down_bf"], final_g)


def _layer_params(l, heads, norm1_g, w_in, conv_a_w, conv_b_w, conv_b_b, ln_b_g, ln_b_b, conv_qkv_w, a_log,
                  dt_bias, gdn_norm_g, w_out, norm2_g, w_gate_up, w_down):
    d, d_in = w_in.shape[1:]
    pad_small = lambda v: jnp.pad(v[l], (0, LANES - heads)).reshape(1, LANES)
    return {
        "norm1_g": norm1_g[l].reshape(1, d),
        "w_in_pad": jnp.pad(w_in[l], ((0, 0), (0, LANES - 2 * heads))).astype(BF16),
        "conv_a_w": conv_a_w[l], "conv_b_w": conv_b_w[l], "conv_b_b": conv_b_b[l].reshape(1, -1),
        "ln_b_g": ln_b_g[l].reshape(1, -1), "ln_b_b": ln_b_b[l].reshape(1, -1), "conv_qkv_w": conv_qkv_w[l],
        "a_log_pad": pad_small(a_log), "dt_bias_pad": pad_small(dt_bias),
        "gdn_norm_g": gdn_norm_g[l].reshape(1, -1), "w_out_bf": w_out[l].astype(BF16),
        "norm2_g": norm2_g[l].reshape(1, d), "w_gate_up_bf": w_gate_up[l].astype(BF16),
        "w_down_bf": w_down[l].astype(BF16),
    }


def _trunk(x, mods, layers, states, final_g):
    depth = len(layers)
    new = [[], [], [], []]
    for l, p in enumerate(layers):
        mod = mods[l][:, None, :]
        outs = _mixer_call(x, mod, p, *(st[l] for st in states))
        x = _ffn_call(outs[0], mod, p, final_g, l == depth - 1)
        for acc, o in zip(new, outs[1:]):
            acc.append(o)
    return (x,) + tuple(jnp.stack(n) for n in new)


def kernel(x_prompt, x_sample, state_conv_a, state_conv_b, state_conv_qkv, state_gdn, c_prompt, c_sample, norm1_g, w_ada, b_ada, w_in, conv_a_w, conv_b_w, conv_b_b, ln_b_g, ln_b_b, conv_qkv_w, a_log, dt_bias, gdn_norm_g, w_out, norm2_g, w_gate_up, w_down, final_norm_g):
    depth = w_in.shape[0]
    heads = a_log.shape[-1]
    bp = x_prompt.shape[0]
    layers = [_layer_params(l, heads, norm1_g, w_in, conv_a_w, conv_b_w, conv_b_b, ln_b_g, ln_b_b, conv_qkv_w,
                            a_log, dt_bias, gdn_norm_g, w_out, norm2_g, w_gate_up, w_down) for l in range(depth)]
    mods = _ada_call(jnp.concatenate([c_prompt, c_sample], axis=0), w_ada, b_ada)
    final_g = final_norm_g.reshape(1, -1)
    zero_states = tuple(jnp.zeros((depth, bp) + st.shape[2:], F32)
                        for st in (state_conv_a, state_conv_b, state_conv_qkv, state_gdn))
    y_p, pa, pb, pq, ps = _trunk(x_prompt, mods[:, :bp], layers, zero_states, final_g)
    y_s, sa, sb, sq, ss = _trunk(x_sample, mods[:, bp:], layers,
                                 (state_conv_a, state_conv_b, state_conv_qkv, state_gdn), final_g)
    return (y_p, y_s, pa, pb, pq, ps, sa, sb, sq, ss)
```

```python
import functools

import jax
import jax.numpy as jnp
from jax import lax
from jax.experimental import pallas as pl
from jax.experimental.pallas import tpu as pltpu

F32 = jnp.float32
BF16 = jnp.bfloat16
EPS = 1e-6
CHUNK = 64
LANES = 128
SUBLANES = 8
HALO_A, HALO_B, HALO_Q = 8, 32, 8
VMEM_LIMIT_BYTES = 56 * 1024 * 1024
MAX_TILE_ROWS = 512
STREAMS_PER_STEP = 4

NN = ((1,), (0,))
NT = ((1,), (1,))
TN = ((0,), (0,))


def _sigmoid(x):
    return 0.5 * jnp.tanh(0.5 * x) + 0.5


def _silu(x):
    return x * _sigmoid(x)


def _softplus(x):
    return jnp.maximum(x, 0.0) + jnp.log1p(jnp.exp(-jnp.abs(x)))


def _dot(a, b, dims=NN):
    return lax.dot_general(a, b, (dims, ((), ())), preferred_element_type=F32)


def _bdot(a, b, dims=NN):
    return _dot(a.astype(BF16), b.astype(BF16), dims)


def _causal_taps(win, w, halo, residues=range(SUBLANES)):
    taps, n = w.shape[0], win.shape[0]
    first = halo - taps + 1
    acc = None
    for r in residues:
        offs = [o for o in range(first, halo + 1) if o % SUBLANES == r]
        if not offs:
            continue
        wr = win if r == 0 else pltpu.roll(win, n - r, axis=0)
        for o in offs:
            term = w[o - first:o - first + 1] * wr[o - r:o - r + CHUNK]
            acc = term if acc is None else acc + term
    return acc


def _rms_scale(x):
    return lax.rsqrt(jnp.mean(x * x, axis=-1, keepdims=True) + EPS)


def _cumsum_rows(x, row_idx):
    k = 1
    while k < x.shape[0]:
        x = x + jnp.where(row_idx >= k, pltpu.roll(x, k, axis=0), 0.0)
        k *= 2
    return x


def _ada_kernel(c_ref, w_ref, b_ref, o_ref):
    cm = _silu(c_ref[...])
    o_ref[0] = _dot(cm.astype(BF16), w_ref[0].astype(BF16)) + b_ref[0]


def _ada_call(c_all, w_ada, b_ada):
    depth, d, n6 = w_ada.shape
    rows = c_all.shape[0]
    tn = n6 // 4
    return pl.pallas_call(
        _ada_kernel,
        grid=(depth, n6 // tn),
        in_specs=[
            pl.BlockSpec((rows, d), lambda l, j: (0, 0)),
            pl.BlockSpec((1, d, tn), lambda l, j: (l, 0, j)),
            pl.BlockSpec((1, 1, tn), lambda l, j: (l, 0, j)),
        ],
        out_specs=pl.BlockSpec((1, rows, tn), lambda l, j: (l, 0, j)),
        out_shape=jax.ShapeDtypeStruct((depth, rows, n6), F32),
        compiler_params=pltpu.CompilerParams(
            dimension_semantics=("arbitrary", "arbitrary"), vmem_limit_bytes=VMEM_LIMIT_BYTES),
        name="ada_mod",
    )(c_all, w_ada, b_ada.reshape(depth, 1, n6))


def _rows(start, n):
    return pl.ds(start if isinstance(start, int) else pl.multiple_of(start, SUBLANES), n)


def _tri_inverse(mats, masks, tick):
    eye, m8, m16, m32 = masks
    a8 = [jnp.where(m8, a, 0.0) for a in mats]
    a2b = [_bdot(m, m).astype(BF16) for m in a8]
    tick()
    a4 = [_dot(b, b).astype(BF16) for b in a2b]
    p = [eye - m for m in a8]
    p = [m + _dot(m.astype(BF16), b) for m, b in zip(p, a2b)]
    tick()
    x = [m + _dot(m.astype(BF16), b) for m, b in zip(p, a4)]
    tick()
    prev = m8
    for m in (m16, m32, None):
        sel = jnp.logical_not(prev) if m is None else jnp.logical_and(m, jnp.logical_not(prev))
        xb = [v.astype(BF16) for v in x]
        ex = [_dot(jnp.where(sel, a, 0.0).astype(BF16), b).astype(BF16) for a, b in zip(mats, xb)]
        tick()
        x = [v - _dot(b, e) for v, b, e in zip(x, xb, ex)]
        tick()
        prev = m
    return x


def _mixer_kernel(dims, x_ref, xn_ref, mod_ref, n1g_ref, win_ref, caw_ref, cbw_ref, cbb_ref, lng_ref, lnb_ref,
                  cqw_ref, alog_ref, dtb_ref, gng_ref, wout_ref, sta_ref, stb_ref, stq_ref, sts_ref,
                  o_ref, na_ref, nb_ref, nq_ref, ns_ref,
                  abuf, ba_scr, bbuf, qbuf, zg_scr, gb_scr, f_qkv, f_yab, f_gam):
    sb, tt, n_tiles, d, d_a, d_b, d_c, heads, taps_a, taps_b, taps_q = dims
    dk = d_c // heads
    o_b = 3 * d_a
    o_q = o_b + 2 * d_b
    o_g = o_q + 3 * d_c
    o_s = o_g + d_c
    cpt = tt // CHUNK
    grp = min(sb, STREAMS_PER_STEP)
    assert sb % grp == 0
    n_sg = sb // grp
    pairs = [(j, hd) for j in range(grp) for hd in range(heads)]
    t = pl.program_id(1)
    n_t = pl.num_programs(1)
    par = lax.rem(t, 2)

    n1g = n1g_ref[...]
    caw = caw_ref[...]
    cbw = cbw_ref[...]
    cqw = cqw_ref[...]
    gng = gng_ref[...]
    ri = lax.broadcasted_iota(jnp.int32, (CHUNK, CHUNK), 0)
    ci = lax.broadcasted_iota(jnp.int32, (CHUNK, CHUNK), 1)
    incl = ri >= ci
    strict = ri > ci
    eye = jnp.where(ri == ci, 1.0, 0.0).astype(F32)
    masks = (eye,) + tuple((ri // b) == (ci // b) for b in (8, 16, 32))
    row_idx = lax.broadcasted_iota(jnp.int32, (CHUNK, LANES), 0)
    is_g_lane = lax.broadcasted_iota(jnp.int32, (CHUNK, LANES), 1) < heads

    def adaln(x, m):
        return ((x * _rms_scale(x) * n1g) * (1.0 + m[:, d:2 * d]) + m[:, 0:d]).astype(BF16)

    def proj_pieces(hb, dst, places, n):
        def put(ref, halo, vals, cols=None):
            for s, r0, off in places:
                if cols is None:
                    ref[dst, s, _rows(halo + r0, n), :] = vals[off:off + n]
                else:
                    ref[dst, s, _rows(halo + r0, n), cols[0]:cols[1]] = vals[off:off + n]

        tile = 2 * LANES

        def piece_a():
            h_a = _dot(hb, win_ref[:, 0:d_a])
            c_a = _dot(hb, win_ref[:, 2 * d_a:3 * d_a])
            put(abuf, HALO_A, c_a * h_a)

        def piece_ba():
            put(ba_scr, 0, _dot(hb, win_ref[:, d_a:2 * d_a]))

        def piece_b():
            z = _dot(hb, win_ref[:, o_b:o_q])
            put(bbuf, HALO_B, z[:, 0:d_b] * _sigmoid(z[:, d_b:2 * d_b]))

        def piece_cols(ref, halo, base, c0):
            def run():
                put(ref, halo, _dot(hb, win_ref[:, c0:c0 + tile]), (c0 - base, c0 - base + tile))
            return run

        def piece_s():
            z = _dot(hb, win_ref[:, o_s:o_s + LANES])
            lane = lax.broadcasted_iota(jnp.int32, z.shape, 1)
            g_all = -jnp.exp(alog_ref[...]) * _softplus(z + dtb_ref[...])
            put(gb_scr, 0, jnp.where(lane < heads, g_all, _sigmoid(z)))

        pieces = [piece_a, piece_ba, piece_b]
        pieces += [piece_cols(qbuf, HALO_Q, o_q, c0) for c0 in range(o_q, o_g, tile)]
        pieces += [piece_cols(zg_scr, 0, o_g, c0) for c0 in range(o_g, o_s, tile)]
        return pieces + [piece_s]

    def front_pieces(src, s, r0, slot):
        def load_b():
            return bbuf[src, s, _rows(r0, CHUNK + HALO_B), :]
        part = {}

        def conv_b_low():
            part["b"] = _causal_taps(load_b(), cbw, HALO_B, range(0, SUBLANES // 2))

        def conv_ab():
            conv = part["b"] + _causal_taps(load_b(), cbw, HALO_B, range(SUBLANES // 2, SUBLANES)) + cbb_ref[...]
            xc = conv - jnp.mean(conv, axis=-1, keepdims=True)
            ln = xc * lax.rsqrt(jnp.mean(xc * xc, axis=-1, keepdims=True) + EPS) * lng_ref[...] + lnb_ref[...]
            conv = _causal_taps(abuf[src, s, _rows(r0, CHUNK + HALO_A), :], caw, HALO_A)
            y_a = ba_scr[src, s, _rows(r0, CHUNK), :] * conv
            f_yab[s, slot] = jnp.concatenate([y_a, _silu(ln)], axis=-1).astype(BF16)

        def conv_qkv(j):
            def run():
                c0 = j * d_c
                win = qbuf[src, s, _rows(r0, CHUNK + HALO_Q), c0:c0 + d_c]
                act = _silu(_causal_taps(win, cqw[:, c0:c0 + d_c], HALO_Q))
                if j < 2:
                    scale = (dk ** -0.5) if j == 0 else 1.0
                    cols = [act[:, hd * dk:(hd + 1) * dk] for hd in range(heads)]
                    act = jnp.concatenate(
                        [m * (lax.rsqrt(jnp.sum(m * m, axis=-1, keepdims=True) + EPS) * scale) for m in cols], axis=-1)
                f_qkv[s, slot, :, c0:c0 + d_c] = act
            return run

        def decay_sum():
            gbv = gb_scr[src, s, _rows(r0, CHUNK), :]
            f_gam[s, slot] = _cumsum_rows(jnp.where(is_g_lane, gbv, 0.0), row_idx)

        return [conv_b_low, conv_ab, conv_qkv(0), conv_qkv(1), conv_qkv(2), decay_sum]

    def back_load(streams, r0, slot):
        return dict(
            s_old=[ns_ref[streams[j], hd] for j, hd in pairs],
            qkv=[f_qkv[s, slot] for s in streams],
            gam=[f_gam[s, slot] for s in streams],
            yab=[f_yab[s, slot] for s in streams],
            gb=[gb_scr[par, s, _rows(r0, CHUNK), :] for s in streams],
            zg=[zg_scr[par, s, _rows(r0, CHUNK), :] for s in streams],
        )

    def back_compute(v_in, tick):
        s_old, qkv, gam, gb, zg = v_in["s_old"], v_in["qkv"], v_in["gam"], v_in["gb"], v_in["zg"]
        gam_t = [g.T for g in gam]
        q = [qkv[j][:, hd * dk:(hd + 1) * dk] for j, hd in pairs]
        k = [qkv[j][:, d_c + hd * dk:d_c + (hd + 1) * dk] for j, hd in pairs]
        v = [qkv[j][:, 2 * d_c + hd * dk:2 * d_c + (hd + 1) * dk] for j, hd in pairs]
        g_col = [gam[j][:, hd:hd + 1] for j, hd in pairs]
        g_row = [gam_t[j][hd:hd + 1, :] for j, hd in pairs]
        g_last = [gam[j][CHUNK - 1:CHUNK, hd:hd + 1] for j, hd in pairs]
        beta = [gb[j][:, heads + hd:heads + hd + 1] for j, hd in pairs]
        decay = [jnp.where(incl, jnp.exp(jnp.minimum(gc - gr, 0.0)), 0.0) for gc, gr in zip(g_col, g_row)]
        eg = [jnp.exp(gc) for gc in g_col]
        kb = [m.astype(BF16) for m in k]
        qb = [m.astype(BF16) for m in q]
        kk = [_dot(m, m, NT) for m in kb]
        qk = [_dot(a, b, NT) for a, b in zip(qb, kb)]
        tick()
        a_low = [jnp.where(strict, b * m * dc, 0.0) for b, m, dc in zip(beta, kk, decay)]
        t_inv = _tri_inverse(a_low, masks, tick)
        rhs = [jnp.concatenate([vv * b, kx * (b * e)], axis=-1).astype(BF16)
               for vv, kx, b, e in zip(v, k, beta, eg)]
        sol = [_dot(ti.astype(BF16), r) for ti, r in zip(t_inv, rhs)]
        tick()
        sb16 = [m.astype(BF16) for m in s_old]
        w_s = [_dot(m[:, dk:2 * dk].astype(BF16), st) for m, st in zip(sol, sb16)]
        q_s = [_dot((m * e).astype(BF16), st) for m, e, st in zip(q, eg, sb16)]
        tick()
        vb = [(m[:, 0:dk] - ws).astype(BF16) for m, ws in zip(sol, w_s)]
        o_in = [_dot((m * dc).astype(BF16), vn) for m, dc, vn in zip(qk, decay, vb)]
        k_v = [_dot((kx * jnp.exp(gl - gc)).astype(BF16), vn, TN) for kx, gl, gc, vn in zip(k, g_last, g_col, vb)]
        tick()
        s_new = [st * jnp.exp(gl) + kv for st, gl, kv in zip(s_old, g_last, k_v)]
        o = [a + b for a, b in zip(q_s, o_in)]
        y_c = [m * _rms_scale(m) * gng * _silu(zg[j][:, hd * dk:(hd + 1) * dk]) for m, (j, hd) in zip(o, pairs)]
        return s_new, y_c

    @pl.when(t == 0)
    def _first_tile():
        abuf[0, :, HALO_A - (taps_a - 1):HALO_A, :] = sta_ref[...]
        bbuf[0, :, HALO_B - (taps_b - 1):HALO_B, :] = stb_ref[...]
        qbuf[0, :, HALO_Q - (taps_q - 1):HALO_Q, :] = stq_ref[...]
        ns_ref[...] = sts_ref[...]
        hb = jnp.concatenate([adaln(x_ref[s], mod_ref[s]) for s in range(sb)], axis=0)
        for piece in proj_pieces(hb, 0, [(s, 0, s * tt) for s in range(sb)], tt):
            piece()

        def first_front(s, carry):
            for piece in front_pieces(0, s, 0, 0):
                piece()
            return carry
        lax.fori_loop(0, sb, first_front, 0)

    abuf[1 - par, :, 0:HALO_A, :] = abuf[par, :, tt:tt + HALO_A, :]
    bbuf[1 - par, :, 0:HALO_B, :] = bbuf[par, :, tt:tt + HALO_B, :]
    qbuf[1 - par, :, 0:HALO_Q, :] = qbuf[par, :, tt:tt + HALO_Q, :]

    def chunk_step(i, carry):
        if n_sg == 1:
            sg, c = 0, i
        elif cpt == 1:
            sg, c = i, 0
        else:
            sg, c = i // cpt, lax.rem(i, cpt)
        streams = [sg * grp + j for j in range(grp)]
        r0 = c * CHUNK
        slot = lax.rem(t * cpt + c, 2)
        if cpt == 1:
            c_next, src_next = 0, 1 - par
        else:
            wrap = (c + 1) == cpt
            c_next, src_next = jnp.where(wrap, 0, c + 1), jnp.where(wrap, 1 - par, par)

        v_in = back_load(streams, r0, slot)
        hb = jnp.concatenate([adaln(xn_ref[s, _rows(r0, CHUNK), :], mod_ref[s]) for s in streams], axis=0)
        pieces = proj_pieces(hb, 1 - par, [(s, r0, j * CHUNK) for j, s in enumerate(streams)], CHUNK)
        fronts = [piece for s in streams for piece in front_pieces(src_next, s, c_next * CHUNK, 1 - slot)]
        if n_tiles == 1:
            pieces = []
            if cpt == 1:
                fronts = []
        if cpt == 1:
            for piece in pieces:
                piece()
            pieces = []
        todo_front, todo_proj = iter(fronts), iter(pieces)

        def tick():
            for todo in (todo_front, todo_proj):
                piece = next(todo, None)
                if piece is not None:
                    piece()
        s_new, y_c = back_compute(v_in, tick)
        for piece in todo_proj:
            piece()
        for piece in todo_front:
            piece()

        for (j, hd), m in zip(pairs, s_new):
            ns_ref[streams[j], hd] = m
        ymix = jnp.concatenate(
            [jnp.concatenate([v_in["yab"][j]] + [m.astype(BF16) for m in y_c[j * heads:(j + 1) * heads]], axis=-1)
             for j in range(grp)], axis=0)
        y = _dot(ymix, wout_ref[...])
        for j, s in enumerate(streams):
            g1 = mod_ref[s][:, 2 * d:3 * d]
            o_ref[s, _rows(r0, CHUNK), :] = x_ref[s, _rows(r0, CHUNK), :] + g1 * y[j * CHUNK:(j + 1) * CHUNK]
        return carry

    lax.fori_loop(0, n_sg * cpt, chunk_step, 0)

    @pl.when(t == n_t - 1)
    def _store_state():
        na_ref[...] = abuf[par, :, tt + HALO_A - (taps_a - 1):tt + HALO_A, :]
        nb_ref[...] = bbuf[par, :, tt + HALO_B - (taps_b - 1):tt + HALO_B, :]
        nq_ref[...] = qbuf[par, :, tt + HALO_Q - (taps_q - 1):tt + HALO_Q, :]


def _tiles(n_streams, t_len):
    grp = STREAMS_PER_STEP
    while n_streams % grp or MAX_TILE_ROWS // grp < CHUNK:
        grp //= 2
    tt = min(t_len, MAX_TILE_ROWS // grp)
    assert tt % CHUNK == 0 and t_len % tt == 0
    sb = max(grp, min(n_streams, MAX_TILE_ROWS // tt))
    while n_streams % sb or sb % grp:
        sb -= 1
    return sb, tt


def _full(shape, single_buffer=False):
    if single_buffer:
        return pl.BlockSpec(shape, lambda b, t: (0,) * len(shape), pipeline_mode=pl.Buffered(1))
    return pl.BlockSpec(shape, lambda b, t: (0,) * len(shape))


def _layer_of(stacked, l, single_buffer=False):
    shape = stacked.shape[1:]
    index_map = lambda b, t: (l,) + (0,) * len(shape)
    if single_buffer:
        return pl.BlockSpec((None,) + shape, index_map, pipeline_mode=pl.Buffered(1))
    return pl.BlockSpec((None,) + shape, index_map)


def _mixer_call(x, mod, p, l, st_a, st_b, st_q, st_s):
    n_b, t_len, d = x.shape
    d_a, d_b, d_c3 = p["conv_a_w"].shape[-1], p["conv_b_w"].shape[-1], p["conv_qkv_w"].shape[-1]
    d_c = d_c3 // 3
    heads = st_s.shape[2]
    taps_a, taps_b, taps_q = p["conv_a_w"].shape[0], p["conv_b_w"].shape[0], p["conv_qkv_w"].shape[0]
    assert taps_a - 1 <= HALO_A and taps_b - 1 <= HALO_B and taps_q - 1 <= HALO_Q
    assert d_c // heads == LANES and 2 * heads <= LANES
    sb, tt = _tiles(n_b, t_len)
    n_t = t_len // tt
    dims = (sb, tt, n_t, d, d_a, d_b, d_c, heads, taps_a, taps_b, taps_q)
    win, wout = p["w_in_pad"], p["w_out_bf"]

    def per_stream(shape):
        return pl.BlockSpec((sb,) + shape, lambda b, t: (b,) + (0,) * len(shape))

    def layer_state(shape):
        return pl.BlockSpec((None, sb) + shape, lambda b, t: (l, b) + (0,) * len(shape))

    x_spec = pl.BlockSpec((sb, tt, d), lambda b, t: (b, t, 0))
    x_next_spec = pl.BlockSpec((sb, tt, d), lambda b, t: (b, jnp.minimum(t + 1, n_t - 1), 0))
    state_shapes = [st_a.shape[2:], st_b.shape[2:], st_q.shape[2:], st_s.shape[2:]]
    in_specs = [x_spec, x_next_spec, per_stream((1, mod.shape[-1])), _full((1, d)), _layer_of(win, l, True),
                _full(p["conv_a_w"].shape), _full(p["conv_b_w"].shape), _full((1, d_b)), _full((1, d_b)),
                _full((1, d_b)), _full(p["conv_qkv_w"].shape), _full((1, LANES)), _full((1, LANES)),
                _full((1, LANES)), _layer_of(wout, l, True)] + [layer_state(s) for s in state_shapes]
    out_specs = [x_spec] + [per_stream(s) for s in state_shapes]
    out_shape = [jax.ShapeDtypeStruct(x.shape, F32)] + [jax.ShapeDtypeStruct(a.shape[1:], F32) for a in (st_a, st_b, st_q, st_s)]
    scratch = [
        pltpu.VMEM((2, sb, tt + HALO_A, d_a), F32),
        pltpu.VMEM((2, sb, tt, d_a), F32),
        pltpu.VMEM((2, sb, tt + HALO_B, d_b), F32),
        pltpu.VMEM((2, sb, tt + HALO_Q, d_c3), F32),
        pltpu.VMEM((2, sb, tt, d_c), F32),
        pltpu.VMEM((2, sb, tt, LANES), F32),
        pltpu.VMEM((sb, 2, CHUNK, d_c3), F32),
        pltpu.VMEM((sb, 2, CHUNK, d_a + d_b), BF16),
        pltpu.VMEM((sb, 2, CHUNK, LANES), F32),
    ]
    return pl.pallas_call(
        functools.partial(_mixer_kernel, dims),
        grid=(n_b // sb, n_t),
        in_specs=in_specs,
        out_specs=out_specs,
        out_shape=out_shape,
        scratch_shapes=scratch,
        compiler_params=pltpu.CompilerParams(
            dimension_semantics=("arbitrary", "arbitrary"), vmem_limit_bytes=VMEM_LIMIT_BYTES),
        name="mixer",
    )(x, x, mod, p["norm1_g"], win, p["conv_a_w"], p["conv_b_w"], p["conv_b_b"], p["ln_b_g"], p["ln_b_b"],
      p["conv_qkv_w"], p["a_log_pad"], p["dt_bias_pad"], p["gdn_norm_g"], wout, st_a, st_b, st_q, st_s)


def _ffn_kernel(dims, x_ref, mod_ref, n2g_ref, wgu_ref, wd_ref, fg_ref, o_ref):
    sb, tt, d, ff, ff_chunks, final_norm = dims
    n2g = n2g_ref[...]
    halves = [range(0, sb // 2), range(sb // 2, sb)] if sb % 2 == 0 else [range(sb)]
    hbs = []
    for half in halves:
        hs = []
        for s in half:
            x = x_ref[s]
            m = mod_ref[s]
            hs.append(((x * _rms_scale(x) * n2g) * (1.0 + m[:, 4 * d:5 * d]) + m[:, 3 * d:4 * d]).astype(BF16))
        hbs.append(jnp.concatenate(hs, axis=0))
    accs = [None] * len(halves)
    for c0, c1 in ff_chunks:
        for i, hb in enumerate(hbs):
            gate = _dot(hb, wgu_ref[:, c0:c1])
            up = _dot(hb, wgu_ref[:, ff + c0:ff + c1])
            part = _dot((_silu(gate) * up).astype(BF16), wd_ref[c0:c1, :])
            accs[i] = part if accs[i] is None else accs[i] + part
    for half, acc in zip(halves, accs):
        for j, s in enumerate(half):
            out = x_ref[s] + mod_ref[s][:, 5 * d:6 * d] * acc[j * tt:(j + 1) * tt]
            if final_norm:
                out = out * _rms_scale(out) * fg_ref[...]
            o_ref[s] = out


def _ffn_call(x, mod, p, l, final_g, final_norm):
    n_b, t_len, d = x.shape
    ff = p["w_down_bf"].shape[1]
    sb, tt = _tiles(n_b, t_len)
    n_chunks = -(-ff // 1024)
    step = -(-ff // (n_chunks * 2 * LANES)) * 2 * LANES
    ff_chunks = tuple((c0, min(c0 + step, ff)) for c0 in range(0, ff, step))
    dims = (sb, tt, d, ff, ff_chunks, final_norm)
    x_spec = pl.BlockSpec((sb, tt, d), lambda b, t: (b, t, 0))
    return pl.pallas_call(
        functools.partial(_ffn_kernel, dims),
        grid=(n_b // sb, t_len // tt),
        in_specs=[x_spec, pl.BlockSpec((sb, 1, mod.shape[-1]), lambda b, t: (b, 0, 0)), _full((1, d)),
                  _layer_of(p["w_gate_up_bf"], l, True), _layer_of(p["w_down_bf"], l, True), _full((1, d))],
        out_specs=x_spec,
        out_shape=jax.ShapeDtypeStruct(x.shape, F32),
        compiler_params=pltpu.CompilerParams(
            dimension_semantics=("arbitrary", "arbitrary"), vmem_limit_bytes=VMEM_LIMIT_BYTES),
        name="ffn",
    )(x, mod, p["norm2_g"], p["w_gate_up_bf"], p["w_down_bf"], final_g)


def _layer_params(l, heads, stacked, norm1_g, conv_a_w, conv_b_w, conv_b_b, ln_b_g, ln_b_b, conv_qkv_w, a_log,
                  dt_bias, gdn_norm_g, norm2_g):
    d = norm1_g.shape[-1]
    pad_small = lambda v: jnp.pad(v[l], (0, LANES - heads)).reshape(1, LANES)
    return {
        **stacked,
        "norm1_g": norm1_g[l].reshape(1, d),
        "conv_a_w": conv_a_w[l], "conv_b_w": conv_b_w[l], "conv_b_b": conv_b_b[l].reshape(1, -1),
        "ln_b_g": ln_b_g[l].reshape(1, -1), "ln_b_b": ln_b_b[l].reshape(1, -1), "conv_qkv_w": conv_qkv_w[l],
        "a_log_pad": pad_small(a_log), "dt_bias_pad": pad_small(dt_bias),
        "gdn_norm_g": gdn_norm_g[l].reshape(1, -1), "norm2_g": norm2_g[l].reshape(1, d),
    }


def _trunk(x, mods, layers, states, final_g):
    depth = len(layers)
    new = [[], [], [], []]
    for l, p in enumerate(layers):
        mod = mods[l][:, None, :]
        outs = _mixer_call(x, mod, p, l, *states)
        x = _ffn_call(outs[0], mod, p, l, final_g, l == depth - 1)
        for acc, o in zip(new, outs[1:]):
            acc.append(o)
    return (x,) + tuple(jnp.stack(n) for n in new)


def kernel(x_prompt, x_sample, state_conv_a, state_conv_b, state_conv_qkv, state_gdn, c_prompt, c_sample, norm1_g, w_ada, b_ada, w_in, conv_a_w, conv_b_w, conv_b_b, ln_b_g, ln_b_b, conv_qkv_w, a_log, dt_bias, gdn_norm_g, w_out, norm2_g, w_gate_up, w_down, final_norm_g):
    depth = w_in.shape[0]
    heads = a_log.shape[-1]
    bp = x_prompt.shape[0]
    stacked = {
        "w_in_pad": jnp.pad(w_in, ((0, 0), (0, 0), (0, LANES - 2 * heads))).astype(BF16),
        "w_out_bf": w_out.astype(BF16), "w_gate_up_bf": w_gate_up.astype(BF16), "w_down_bf": w_down.astype(BF16),
    }
    layers = [_layer_params(l, heads, stacked, norm1_g, conv_a_w, conv_b_w, conv_b_b, ln_b_g, ln_b_b, conv_qkv_w,
                            a_log, dt_bias, gdn_norm_g, norm2_g) for l in range(depth)]
    mods = _ada_call(jnp.concatenate([c_prompt, c_sample], axis=0), w_ada, b_ada)
    final_g = final_norm_g.reshape(1, -1)
    zero_states = tuple(jnp.zeros((depth, bp) + st.shape[2:], F32)
                        for st in (state_conv_a, state_conv_b, state_conv_qkv, state_gdn))
    y_p, pa, pb, pq, ps = _trunk(x_prompt, mods[:, :bp], layers, zero_states, final_g)
    y_s, sa, sb, sq, ss = _trunk(x_sample, mods[:, bp:], layers,
                                 (state_conv_a, state_conv_b, state_conv_qkv, state_gdn), final_g)
    return (y_p, y_s, pa, pb, pq, ps, sa, sb, sq, ss)
```

```python
import functools

import jax
import jax.numpy as jnp
from jax import lax
from jax.experimental import pallas as pl
from jax.experimental.pallas import tpu as pltpu

F32 = jnp.float32
BF16 = jnp.bfloat16
EPS = 1e-6
CHUNK = 64
LANES = 128
SUBLANES = 8
HALO_A, HALO_B, HALO_Q = 8, 32, 8
VMEM_LIMIT_BYTES = 56 * 1024 * 1024
MAX_TILE_ROWS = 512
STREAMS_PER_STEP = 4

NN = ((1,), (0,))
NT = ((1,), (1,))
TN = ((0,), (0,))


def _sigmoid(x):
    return 0.5 * jnp.tanh(0.5 * x) + 0.5


def _silu(x):
    return x * _sigmoid(x)


def _softplus(x):
    return jnp.maximum(x, 0.0) + jnp.log1p(jnp.exp(-jnp.abs(x)))


def _dot(a, b, dims=NN):
    return lax.dot_general(a, b, (dims, ((), ())), preferred_element_type=F32)


def _bdot(a, b, dims=NN):
    return _dot(a.astype(BF16), b.astype(BF16), dims)


def _causal_taps(win, w, halo, residues=range(SUBLANES)):
    taps, n = w.shape[0], win.shape[0]
    first = halo - taps + 1
    acc = None
    for r in residues:
        offs = [o for o in range(first, halo + 1) if o % SUBLANES == r]
        if not offs:
            continue
        wr = win if r == 0 else pltpu.roll(win, n - r, axis=0)
        for o in offs:
            term = w[o - first:o - first + 1] * wr[o - r:o - r + CHUNK]
            acc = term if acc is None else acc + term
    return acc


def _rms_scale(x):
    return lax.rsqrt(jnp.mean(x * x, axis=-1, keepdims=True) + EPS)


def _cumsum_rows(x, row_idx):
    k = 1
    while k < x.shape[0]:
        x = x + jnp.where(row_idx >= k, pltpu.roll(x, k, axis=0), 0.0)
        k *= 2
    return x


def _ada_kernel(c_ref, w_ref, b_ref, o_ref):
    cm = _silu(c_ref[...])
    o_ref[0] = _dot(cm.astype(BF16), w_ref[0].astype(BF16)) + b_ref[0]


def _ada_call(c_all, w_ada, b_ada):
    depth, d, n6 = w_ada.shape
    rows = c_all.shape[0]
    tn = n6 // 4
    return pl.pallas_call(
        _ada_kernel,
        grid=(depth, n6 // tn),
        in_specs=[
            pl.BlockSpec((rows, d), lambda l, j: (0, 0)),
            pl.BlockSpec((1, d, tn), lambda l, j: (l, 0, j)),
            pl.BlockSpec((1, 1, tn), lambda l, j: (l, 0, j)),
        ],
        out_specs=pl.BlockSpec((1, rows, tn), lambda l, j: (l, 0, j)),
        out_shape=jax.ShapeDtypeStruct((depth, rows, n6), F32),
        compiler_params=pltpu.CompilerParams(
            dimension_semantics=("arbitrary", "arbitrary"), vmem_limit_bytes=VMEM_LIMIT_BYTES),
        name="ada_mod",
    )(c_all, w_ada, b_ada.reshape(depth, 1, n6))


def _rows(start, n):
    return pl.ds(start if isinstance(start, int) else pl.multiple_of(start, SUBLANES), n)


def _pair_dot(x, z, low_lanes):
    zero = jnp.zeros_like(z)
    return _dot(x, jnp.concatenate([jnp.where(low_lanes, z, zero), jnp.where(low_lanes, zero, z)], axis=0))


def _tri_inverse(mats, masks, tick):
    eye, m8, m16, m32, low = masks
    a8 = [jnp.where(m8, a, 0.0) for a in mats]
    a8b = [m.astype(BF16) for m in a8]
    a2b = [_pair_dot(b, b, low).astype(BF16) for b in a8b]
    tick()
    a4 = [_pair_dot(b, b, low).astype(BF16) for b in a2b]
    p = [eye - m for m in a8]
    p = [m + _pair_dot(m.astype(BF16), b, low) for m, b in zip(p, a2b)]
    tick()
    x = [m + _pair_dot(m.astype(BF16), b, low) for m, b in zip(p, a4)]
    tick()
    prev = m8
    for m in (m16, m32, None):
        sel = jnp.logical_not(prev) if m is None else jnp.logical_and(m, jnp.logical_not(prev))
        xb = [v.astype(BF16) for v in x]
        ex = [_pair_dot(jnp.where(sel, a, 0.0).astype(BF16), b, low).astype(BF16) for a, b in zip(mats, xb)]
        tick()
        x = [v - _pair_dot(b, e, low) for v, b, e in zip(x, xb, ex)]
        tick()
        prev = m
    return x


def _mixer_kernel(dims, x_ref, xn_ref, mod_ref, n1g_ref, win_ref, caw_ref, cbw_ref, cbb_ref, lng_ref, lnb_ref,
                  cqw_ref, alog_ref, dtb_ref, gng_ref, wout_ref, sta_ref, stb_ref, stq_ref, sts_ref,
                  o_ref, na_ref, nb_ref, nq_ref, ns_ref,
                  abuf, ba_scr, bbuf, qbuf, zg_scr, gb_scr, f_qkv, f_yab, f_gam):
    sb, tt, n_tiles, d, d_a, d_b, d_c, heads, taps_a, taps_b, taps_q = dims
    dk = d_c // heads
    o_b = 3 * d_a
    o_q = o_b + 2 * d_b
    o_g = o_q + 3 * d_c
    o_s = o_g + d_c
    cpt = tt // CHUNK
    grp = min(sb, STREAMS_PER_STEP)
    assert sb % grp == 0
    n_sg = sb // grp
    pairs = [(j, hd) for j in range(grp) for hd in range(heads)]
    hpairs = [(j, hp) for j in range(grp) for hp in range(heads // 2)]
    t = pl.program_id(1)
    n_t = pl.num_programs(1)
    par = lax.rem(t, 2)

    n1g = n1g_ref[...]
    caw = caw_ref[...]
    cbw = cbw_ref[...]
    cqw = cqw_ref[...]
    gng = gng_ref[...]
    assert 2 * CHUNK == LANES and heads % 2 == 0
    ri = lax.broadcasted_iota(jnp.int32, (CHUNK, LANES), 0)
    lane = lax.broadcasted_iota(jnp.int32, (CHUNK, LANES), 1)
    low = lane < CHUNK
    ci = jnp.where(low, lane, lane - CHUNK)
    incl = ri >= ci
    strict = ri > ci
    eye = jnp.where(ri == ci, 1.0, 0.0).astype(F32)
    masks = (eye,) + tuple((ri // b) == (ci // b) for b in (8, 16, 32)) + (low,)
    row_idx = lax.broadcasted_iota(jnp.int32, (CHUNK, LANES), 0)
    is_g_lane = lax.broadcasted_iota(jnp.int32, (CHUNK, LANES), 1) < heads

    def adaln(x, m):
        return ((x * _rms_scale(x) * n1g) * (1.0 + m[:, d:2 * d]) + m[:, 0:d]).astype(BF16)

    def proj_pieces(hb, dst, places, n):
        def put(ref, halo, vals, cols=None):
            for s, r0, off in places:
                if cols is None:
                    ref[dst, s, _rows(halo + r0, n), :] = vals[off:off + n]
                else:
                    ref[dst, s, _rows(halo + r0, n), cols[0]:cols[1]] = vals[off:off + n]

        tile = 2 * LANES

        def piece_a():
            h_a = _dot(hb, win_ref[:, 0:d_a])
            c_a = _dot(hb, win_ref[:, 2 * d_a:3 * d_a])
            put(abuf, HALO_A, c_a * h_a)

        def piece_ba():
            put(ba_scr, 0, _dot(hb, win_ref[:, d_a:2 * d_a]))

        def piece_b():
            z = _dot(hb, win_ref[:, o_b:o_q])
            put(bbuf, HALO_B, z[:, 0:d_b] * _sigmoid(z[:, d_b:2 * d_b]))

        def piece_cols(ref, halo, base, c0):
            def run():
                put(ref, halo, _dot(hb, win_ref[:, c0:c0 + tile]), (c0 - base, c0 - base + tile))
            return run

        def piece_s():
            z = _dot(hb, win_ref[:, o_s:o_s + LANES])
            lane = lax.broadcasted_iota(jnp.int32, z.shape, 1)
            g_all = -jnp.exp(alog_ref[...]) * _softplus(z + dtb_ref[...])
            put(gb_scr, 0, jnp.where(lane < heads, g_all, _sigmoid(z)))

        pieces = [piece_a, piece_ba, piece_b]
        pieces += [piece_cols(qbuf, HALO_Q, o_q, c0) for c0 in range(o_q, o_g, tile)]
        pieces += [piece_cols(zg_scr, 0, o_g, c0) for c0 in range(o_g, o_s, tile)]
        return pieces + [piece_s]

    def front_pieces(src, s, r0, slot):
        def load_b():
            return bbuf[src, s, _rows(r0, CHUNK + HALO_B), :]
        part = {}

        def conv_b_low():
            part["b"] = _causal_taps(load_b(), cbw, HALO_B, range(0, SUBLANES // 2))

        def conv_ab():
            conv = part["b"] + _causal_taps(load_b(), cbw, HALO_B, range(SUBLANES // 2, SUBLANES)) + cbb_ref[...]
            xc = conv - jnp.mean(conv, axis=-1, keepdims=True)
            ln = xc * lax.rsqrt(jnp.mean(xc * xc, axis=-1, keepdims=True) + EPS) * lng_ref[...] + lnb_ref[...]
            conv = _causal_taps(abuf[src, s, _rows(r0, CHUNK + HALO_A), :], caw, HALO_A)
            y_a = ba_scr[src, s, _rows(r0, CHUNK), :] * conv
            f_yab[s, slot] = jnp.concatenate([y_a, _silu(ln)], axis=-1).astype(BF16)

        def conv_qkv(j):
            def run():
                c0 = j * d_c
                win = qbuf[src, s, _rows(r0, CHUNK + HALO_Q), c0:c0 + d_c]
                act = _silu(_causal_taps(win, cqw[:, c0:c0 + d_c], HALO_Q))
                if j < 2:
                    scale = (dk ** -0.5) if j == 0 else 1.0
                    cols = [act[:, hd * dk:(hd + 1) * dk] for hd in range(heads)]
                    act = jnp.concatenate(
                        [m * (lax.rsqrt(jnp.sum(m * m, axis=-1, keepdims=True) + EPS) * scale) for m in cols], axis=-1)
                f_qkv[s, slot, :, c0:c0 + d_c] = act
            return run

        def decay_sum():
            gbv = gb_scr[src, s, _rows(r0, CHUNK), :]
            f_gam[s, slot] = _cumsum_rows(jnp.where(is_g_lane, gbv, 0.0), row_idx)

        return [conv_b_low, conv_ab, conv_qkv(0), conv_qkv(1), conv_qkv(2), decay_sum]

    def back_load(streams, r0, slot):
        return dict(
            s_old=[ns_ref[streams[j], hd] for j, hd in pairs],
            qkv=[f_qkv[s, slot] for s in streams],
            gam=[f_gam[s, slot] for s in streams],
            yab=[f_yab[s, slot] for s in streams],
            gb=[gb_scr[par, s, _rows(r0, CHUNK), :] for s in streams],
            zg=[zg_scr[par, s, _rows(r0, CHUNK), :] for s in streams],
        )

    def back_compute(v_in, tick):
        s_old, qkv, gam, gb, zg = v_in["s_old"], v_in["qkv"], v_in["gam"], v_in["gb"], v_in["zg"]
        gam_t = [g.T for g in gam]
        q = [qkv[j][:, hd * dk:(hd + 1) * dk] for j, hd in pairs]
        k = [qkv[j][:, d_c + hd * dk:d_c + (hd + 1) * dk] for j, hd in pairs]
        v = [qkv[j][:, 2 * d_c + hd * dk:2 * d_c + (hd + 1) * dk] for j, hd in pairs]
        g_col = [gam[j][:, hd:hd + 1] for j, hd in pairs]
        g_last = [gam[j][CHUNK - 1:CHUNK, hd:hd + 1] for j, hd in pairs]
        beta = [gb[j][:, heads + hd:heads + hd + 1] for j, hd in pairs]
        eg = [jnp.exp(gc) for gc in g_col]

        def both(vals, j, hp):
            return jnp.where(low, vals[j * heads + 2 * hp], vals[j * heads + 2 * hp + 1])
        g_row2 = [jnp.concatenate([gam_t[j][2 * hp:2 * hp + 1, :], gam_t[j][2 * hp + 1:2 * hp + 2, :]], axis=-1)
                  for j, hp in hpairs]
        decay2 = [jnp.where(incl, jnp.exp(jnp.minimum(both(g_col, j, hp) - gr, 0.0)), 0.0)
                  for (j, hp), gr in zip(hpairs, g_row2)]
        zero_k = jnp.zeros((CHUNK, dk), BF16)
        k2b = [qkv[j][:, d_c + 2 * hp * dk:d_c + (2 * hp + 2) * dk].astype(BF16) for j, hp in hpairs]
        q2b = [qkv[j][:, 2 * hp * dk:(2 * hp + 2) * dk].astype(BF16) for j, hp in hpairs]
        kbd = [jnp.concatenate([jnp.concatenate([m[:, 0:dk], zero_k], axis=-1),
                                jnp.concatenate([zero_k, m[:, dk:2 * dk]], axis=-1)], axis=0) for m in k2b]
        kk2 = [_dot(m, bd, NT) for m, bd in zip(k2b, kbd)]
        qk2 = [_dot(m, bd, NT) for m, bd in zip(q2b, kbd)]
        tick()
        a_low = [jnp.where(strict, both(beta, j, hp) * m * dc, 0.0) for (j, hp), m, dc in zip(hpairs, kk2, decay2)]
        t_inv = [m.astype(BF16) for m in _tri_inverse(a_low, masks, tick)]
        qkd = [(m * dc).astype(BF16) for m, dc in zip(qk2, decay2)]

        def unpacked(packed, rhs):
            out = []
            for (j, hd), r in zip(pairs, rhs):
                zero = jnp.zeros_like(r)
                stacked = [r, zero] if hd % 2 == 0 else [zero, r]
                out.append(_dot(packed[j * (heads // 2) + hd // 2], jnp.concatenate(stacked, axis=0)))
            return out
        rhs = [jnp.concatenate([vv * b, kx * (b * e)], axis=-1).astype(BF16)
               for vv, kx, b, e in zip(v, k, beta, eg)]
        sol = unpacked(t_inv, rhs)
        tick()
        sb16 = [m.astype(BF16) for m in s_old]
        w_s = [_dot(m[:, dk:2 * dk].astype(BF16), st) for m, st in zip(sol, sb16)]
        q_s = [_dot((m * e).astype(BF16), st) for m, e, st in zip(q, eg, sb16)]
        tick()
        vb = [(m[:, 0:dk] - ws).astype(BF16) for m, ws in zip(sol, w_s)]
        o_in = unpacked(qkd, vb)
        k_v = [_dot((kx * jnp.exp(gl - gc)).astype(BF16), vn, TN) for kx, gl, gc, vn in zip(k, g_last, g_col, vb)]
        tick()
        s_new = [st * jnp.exp(gl) + kv for st, gl, kv in zip(s_old, g_last, k_v)]
        o = [a + b for a, b in zip(q_s, o_in)]
        y_c = [m * _rms_scale(m) * gng * _silu(zg[j][:, hd * dk:(hd + 1) * dk]) for m, (j, hd) in zip(o, pairs)]
        return s_new, y_c

    @pl.when(t == 0)
    def _first_tile():
        abuf[0, :, HALO_A - (taps_a - 1):HALO_A, :] = sta_ref[...]
        bbuf[0, :, HALO_B - (taps_b - 1):HALO_B, :] = stb_ref[...]
        qbuf[0, :, HALO_Q - (taps_q - 1):HALO_Q, :] = stq_ref[...]
        ns_ref[...] = sts_ref[...]
        hb = jnp.concatenate([adaln(x_ref[s], mod_ref[s]) for s in range(sb)], axis=0)
        for piece in proj_pieces(hb, 0, [(s, 0, s * tt) for s in range(sb)], tt):
            piece()

        def first_front(s, carry):
            for piece in front_pieces(0, s, 0, 0):
                piece()
            return carry
        lax.fori_loop(0, sb, first_front, 0)

    abuf[1 - par, :, 0:HALO_A, :] = abuf[par, :, tt:tt + HALO_A, :]
    bbuf[1 - par, :, 0:HALO_B, :] = bbuf[par, :, tt:tt + HALO_B, :]
    qbuf[1 - par, :, 0:HALO_Q, :] = qbuf[par, :, tt:tt + HALO_Q, :]

    def chunk_step(i, carry):
        if n_sg == 1:
            sg, c = 0, i
        elif cpt == 1:
            sg, c = i, 0
        else:
            sg, c = i // cpt, lax.rem(i, cpt)
        streams = [sg * grp + j for j in range(grp)]
        r0 = c * CHUNK
        slot = lax.rem(t * cpt + c, 2)
        if cpt == 1:
            c_next, src_next = 0, 1 - par
        else:
            wrap = (c + 1) == cpt
            c_next, src_next = jnp.where(wrap, 0, c + 1), jnp.where(wrap, 1 - par, par)

        v_in = back_load(streams, r0, slot)
        hb = jnp.concatenate([adaln(xn_ref[s, _rows(r0, CHUNK), :], mod_ref[s]) for s in streams], axis=0)
        pieces = proj_pieces(hb, 1 - par, [(s, r0, j * CHUNK) for j, s in enumerate(streams)], CHUNK)
        fronts = [piece for s in streams for piece in front_pieces(src_next, s, c_next * CHUNK, 1 - slot)]
        if n_tiles == 1:
            pieces = []
            if cpt == 1:
                fronts = []
        if cpt == 1:
            for piece in pieces:
                piece()
            pieces = []
        todo_front, todo_proj = iter(fronts), iter(pieces)

        def tick():
            for todo in (todo_front, todo_proj):
                piece = next(todo, None)
                if piece is not None:
                    piece()
        s_new, y_c = back_compute(v_in, tick)
        for piece in todo_proj:
            piece()
        for piece in todo_front:
            piece()

        for (j, hd), m in zip(pairs, s_new):
            ns_ref[streams[j], hd] = m
        ymix = jnp.concatenate(
            [jnp.concatenate([v_in["yab"][j]] + [m.astype(BF16) for m in y_c[j * heads:(j + 1) * heads]], axis=-1)
             for j in range(grp)], axis=0)
        y = _dot(ymix, wout_ref[...])
        for j, s in enumerate(streams):
            g1 = mod_ref[s][:, 2 * d:3 * d]
            o_ref[s, _rows(r0, CHUNK), :] = x_ref[s, _rows(r0, CHUNK), :] + g1 * y[j * CHUNK:(j + 1) * CHUNK]
        return carry

    lax.fori_loop(0, n_sg * cpt, chunk_step, 0)

    @pl.when(t == n_t - 1)
    def _store_state():
        na_ref[...] = abuf[par, :, tt + HALO_A - (taps_a - 1):tt + HALO_A, :]
        nb_ref[...] = bbuf[par, :, tt + HALO_B - (taps_b - 1):tt + HALO_B, :]
        nq_ref[...] = qbuf[par, :, tt + HALO_Q - (taps_q - 1):tt + HALO_Q, :]


def _tiles(n_streams, t_len):
    grp = STREAMS_PER_STEP
    while n_streams % grp or MAX_TILE_ROWS // grp < CHUNK:
        grp //= 2
    tt = min(t_len, MAX_TILE_ROWS // grp)
    assert tt % CHUNK == 0 and t_len % tt == 0
    sb = max(grp, min(n_streams, MAX_TILE_ROWS // tt))
    while n_streams % sb or sb % grp:
        sb -= 1
    return sb, tt


def _full(shape, single_buffer=False):
    if single_buffer:
        return pl.BlockSpec(shape, lambda b, t: (0,) * len(shape), pipeline_mode=pl.Buffered(1))
    return pl.BlockSpec(shape, lambda b, t: (0,) * len(shape))


def _layer_of(stacked, l, single_buffer=False):
    shape = stacked.shape[1:]
    index_map = lambda b, t: (l,) + (0,) * len(shape)
    if single_buffer:
        return pl.BlockSpec((None,) + shape, index_map, pipeline_mode=pl.Buffered(1))
    return pl.BlockSpec((None,) + shape, index_map)


def _mixer_call(x, mod, p, l, st_a, st_b, st_q, st_s):
    n_b, t_len, d = x.shape
    d_a, d_b, d_c3 = p["conv_a_w"].shape[-1], p["conv_b_w"].shape[-1], p["conv_qkv_w"].shape[-1]
    d_c = d_c3 // 3
    heads = st_s.shape[2]
    taps_a, taps_b, taps_q = p["conv_a_w"].shape[0], p["conv_b_w"].shape[0], p["conv_qkv_w"].shape[0]
    assert taps_a - 1 <= HALO_A and taps_b - 1 <= HALO_B and taps_q - 1 <= HALO_Q
    assert d_c // heads == LANES and 2 * heads <= LANES
    sb, tt = _tiles(n_b, t_len)
    n_t = t_len // tt
    dims = (sb, tt, n_t, d, d_a, d_b, d_c, heads, taps_a, taps_b, taps_q)
    win, wout = p["w_in_pad"], p["w_out_bf"]

    def per_stream(shape):
        return pl.BlockSpec((sb,) + shape, lambda b, t: (b,) + (0,) * len(shape))

    def layer_state(shape):
        return pl.BlockSpec((None, sb) + shape, lambda b, t: (l, b) + (0,) * len(shape))

    x_spec = pl.BlockSpec((sb, tt, d), lambda b, t: (b, t, 0))
    x_next_spec = pl.BlockSpec((sb, tt, d), lambda b, t: (b, jnp.minimum(t + 1, n_t - 1), 0))
    state_shapes = [st_a.shape[2:], st_b.shape[2:], st_q.shape[2:], st_s.shape[2:]]
    in_specs = [x_spec, x_next_spec, per_stream((1, mod.shape[-1])), _full((1, d)), _layer_of(win, l, True),
                _full(p["conv_a_w"].shape), _full(p["conv_b_w"].shape), _full((1, d_b)), _full((1, d_b)),
                _full((1, d_b)), _full(p["conv_qkv_w"].shape), _full((1, LANES)), _full((1, LANES)),
                _full((1, LANES)), _layer_of(wout, l, True)] + [layer_state(s) for s in state_shapes]
    out_specs = [x_spec] + [per_stream(s) for s in state_shapes]
    out_shape = [jax.ShapeDtypeStruct(x.shape, F32)] + [jax.ShapeDtypeStruct(a.shape[1:], F32) for a in (st_a, st_b, st_q, st_s)]
    scratch = [
        pltpu.VMEM((2, sb, tt + HALO_A, d_a), F32),
        pltpu.VMEM((2, sb, tt, d_a), F32),
        pltpu.VMEM((2, sb, tt + HALO_B, d_b), F32),
        pltpu.VMEM((2, sb, tt + HALO_Q, d_c3), F32),
        pltpu.VMEM((2, sb, tt, d_c), F32),
        pltpu.VMEM((2, sb, tt, LANES), F32),
        pltpu.VMEM((sb, 2, CHUNK, d_c3), F32),
        pltpu.VMEM((sb, 2, CHUNK, d_a + d_b), BF16),
        pltpu.VMEM((sb, 2, CHUNK, LANES), F32),
    ]
    return pl.pallas_call(
        functools.partial(_mixer_kernel, dims),
        grid=(n_b // sb, n_t),
        in_specs=in_specs,
        out_specs=out_specs,
        out_shape=out_shape,
        scratch_shapes=scratch,
        compiler_params=pltpu.CompilerParams(
            dimension_semantics=("arbitrary", "arbitrary"), vmem_limit_bytes=VMEM_LIMIT_BYTES),
        name="mixer",
    )(x, x, mod, p["norm1_g"], win, p["conv_a_w"], p["conv_b_w"], p["conv_b_b"], p["ln_b_g"], p["ln_b_b"],
      p["conv_qkv_w"], p["a_log_pad"], p["dt_bias_pad"], p["gdn_norm_g"], wout, st_a, st_b, st_q, st_s)


def _ffn_kernel(dims, x_ref, mod_ref, n2g_ref, wgu_ref, wd_ref, fg_ref, o_ref):
    sb, tt, d, ff, ff_chunks, final_norm = dims
    n2g = n2g_ref[...]
    halves = [range(0, sb // 2), range(sb // 2, sb)] if sb % 2 == 0 else [range(sb)]
    hbs = []
    for half in halves:
        hs = []
        for s in half:
            x = x_ref[s]
            m = mod_ref[s]
            hs.append(((x * _rms_scale(x) * n2g) * (1.0 + m[:, 4 * d:5 * d]) + m[:, 3 * d:4 * d]).astype(BF16))
        hbs.append(jnp.concatenate(hs, axis=0))
    accs = [None] * len(halves)
    for c0, c1 in ff_chunks:
        for i, hb in enumerate(hbs):
            gate = _dot(hb, wgu_ref[:, c0:c1])
            up = _dot(hb, wgu_ref[:, ff + c0:ff + c1])
            part = _dot((_silu(gate) * up).astype(BF16), wd_ref[c0:c1, :])
            accs[i] = part if accs[i] is None else accs[i] + part
    for half, acc in zip(halves, accs):
        for j, s in enumerate(half):
            out = x_ref[s] + mod_ref[s][:, 5 * d:6 * d] * acc[j * tt:(j + 1) * tt]
            if final_norm:
                out = out * _rms_scale(out) * fg_ref[...]
            o_ref[s] = out


def _ffn_call(x, mod, p, l, final_g, final_norm):
    n_b, t_len, d = x.shape
    ff = p["w_down_bf"].shape[1]
    sb, tt = _tiles(n_b, t_len)
    n_chunks = -(-ff // 1024)
    step = -(-ff // (n_chunks * 2 * LANES)) * 2 * LANES
    ff_chunks = tuple((c0, min(c0 + step, ff)) for c0 in range(0, ff, step))
    dims = (sb, tt, d, ff, ff_chunks, final_norm)
    x_spec = pl.BlockSpec((sb, tt, d), lambda b, t: (b, t, 0))
    return pl.pallas_call(
        functools.partial(_ffn_kernel, dims),
        grid=(n_b // sb, t_len // tt),
        in_specs=[x_spec, pl.BlockSpec((sb, 1, mod.shape[-1]), lambda b, t: (b, 0, 0)), _full((1, d)),
                  _layer_of(p["w_gate_up_bf"], l, True), _layer_of(p["w_down_bf"], l, True), _full((1, d))],
        out_specs=x_spec,
        out_shape=jax.ShapeDtypeStruct(x.shape, F32),
        compiler_params=pltpu.CompilerParams(
            dimension_semantics=("arbitrary", "arbitrary"), vmem_limit_bytes=VMEM_LIMIT_BYTES),
        name="ffn",
    )(x, mod, p["norm2_g"], p["w_gate_up_bf"], p["w_down_bf"], final_g)


def _layer_params(l, heads, stacked, norm1_g, conv_a_w, conv_b_w, conv_b_b, ln_b_g, ln_b_b, conv_qkv_w, a_log,
                  dt_bias, gdn_norm_g, norm2_g):
    d = norm1_g.shape[-1]
    pad_small = lambda v: jnp.pad(v[l], (0, LANES - heads)).reshape(1, LANES)
    return {
        **stacked,
        "norm1_g": norm1_g[l].reshape(1, d),
        "conv_a_w": conv_a_w[l], "conv_b_w": conv_b_w[l], "conv_b_b": conv_b_b[l].reshape(1, -1),
        "ln_b_g": ln_b_g[l].reshape(1, -1), "ln_b_b": ln_b_b[l].reshape(1, -1), "conv_qkv_w": conv_qkv_w[l],
        "a_log_pad": pad_small(a_log), "dt_bias_pad": pad_small(dt_bias),
        "gdn_norm_g": gdn_norm_g[l].reshape(1, -1), "norm2_g": norm2_g[l].reshape(1, d),
    }


def _trunk(x, mods, layers, states, final_g):
    depth = len(layers)
    new = [[], [], [], []]
    for l, p in enumerate(layers):
        mod = mods[l][:, None, :]
        outs = _mixer_call(x, mod, p, l, *states)
        x = _ffn_call(outs[0], mod, p, l, final_g, l == depth - 1)
        for acc, o in zip(new, outs[1:]):
            acc.append(o)
    return (x,) + tuple(jnp.stack(n) for n in new)


def kernel(x_prompt, x_sample, state_conv_a, state_conv_b, state_conv_qkv, state_gdn, c_prompt, c_sample, norm1_g, w_ada, b_ada, w_in, conv_a_w, conv_b_w, conv_b_b, ln_b_g, ln_b_b, conv_qkv_w, a_log, dt_bias, gdn_norm_g, w_out, norm2_g, w_gate_up, w_down, final_norm_g):
    depth = w_in.shape[0]
    heads = a_log.shape[-1]
    bp = x_prompt.shape[0]
    stacked = {
        "w_in_pad": jnp.pad(w_in, ((0, 0), (0, 0), (0, LANES - 2 * heads))).astype(BF16),
        "w_out_bf": w_out.astype(BF16), "w_gate_up_bf": w_gate_up.astype(BF16), "w_down_bf": w_down.astype(BF16),
    }
    layers = [_layer_params(l, heads, stacked, norm1_g, conv_a_w, conv_b_w, conv_b_b, ln_b_g, ln_b_b, conv_qkv_w,
                            a_log, dt_bias, gdn_norm_g, norm2_g) for l in range(depth)]
    mods = _ada_call(jnp.concatenate([c_prompt, c_sample], axis=0), w_ada, b_ada)
    final_g = final_norm_g.reshape(1, -1)
    zero_states = tuple(jnp.zeros((depth, bp) + st.shape[2:], F32)
                        for st in (state_conv_a, state_conv_b, state_conv_qkv, state_gdn))
    y_p, pa, pb, pq, ps = _trunk(x_prompt, mods[:, :bp], layers, zero_states, final_g)
    y_s, sa, sb, sq, ss = _trunk(x_sample, mods[:, bp:], layers,
                                 (state_conv_a, state_conv_b, state_conv_qkv, state_gdn), final_g)
    return (y_p, y_s, pa, pb, pq, ps, sa, sb, sq, ss)
```

```python
import functools

import jax
import jax.numpy as jnp
from jax import lax
from jax.experimental import pallas as pl
from jax.experimental.pallas import tpu as pltpu

F32 = jnp.float32
BF16 = jnp.bfloat16
EPS = 1e-6
CHUNK = 64
LANES = 128
SUBLANES = 8
HALO_A, HALO_B, HALO_Q = 8, 32, 8
VMEM_LIMIT_BYTES = 56 * 1024 * 1024
MAX_TILE_ROWS = 512
STREAMS_PER_STEP = 4

NN = ((1,), (0,))
NT = ((1,), (1,))
TN = ((0,), (0,))


def _sigmoid(x):
    return 0.5 * jnp.tanh(0.5 * x) + 0.5


def _silu(x):
    return x * _sigmoid(x)


def _softplus(x):
    return jnp.maximum(x, 0.0) + jnp.log1p(jnp.exp(-jnp.abs(x)))


def _dot(a, b, dims=NN):
    return lax.dot_general(a, b, (dims, ((), ())), preferred_element_type=F32)


def _bdot(a, b, dims=NN):
    return _dot(a.astype(BF16), b.astype(BF16), dims)


def _causal_taps(win, w, halo, residues=range(SUBLANES)):
    taps, n = w.shape[0], win.shape[0]
    first = halo - taps + 1
    acc = None
    for r in residues:
        offs = [o for o in range(first, halo + 1) if o % SUBLANES == r]
        if not offs:
            continue
        wr = win if r == 0 else pltpu.roll(win, n - r, axis=0)
        for o in offs:
            term = w[o - first:o - first + 1] * wr[o - r:o - r + CHUNK]
            acc = term if acc is None else acc + term
    return acc


def _rms_scale(x):
    return lax.rsqrt(jnp.mean(x * x, axis=-1, keepdims=True) + EPS)


def _cumsum_rows(x, row_idx):
    k = 1
    while k < x.shape[0]:
        x = x + jnp.where(row_idx >= k, pltpu.roll(x, k, axis=0), 0.0)
        k *= 2
    return x


def _ada_kernel(c_ref, w_ref, b_ref, o_ref):
    cm = _silu(c_ref[...])
    o_ref[0] = _dot(cm.astype(BF16), w_ref[0].astype(BF16)) + b_ref[0]


def _ada_call(c_all, w_ada, b_ada):
    depth, d, n6 = w_ada.shape
    rows = c_all.shape[0]
    tn = n6 // 4
    return pl.pallas_call(
        _ada_kernel,
        grid=(depth, n6 // tn),
        in_specs=[
            pl.BlockSpec((rows, d), lambda l, j: (0, 0)),
            pl.BlockSpec((1, d, tn), lambda l, j: (l, 0, j)),
            pl.BlockSpec((1, 1, tn), lambda l, j: (l, 0, j)),
        ],
        out_specs=pl.BlockSpec((1, rows, tn), lambda l, j: (l, 0, j)),
        out_shape=jax.ShapeDtypeStruct((depth, rows, n6), F32),
        compiler_params=pltpu.CompilerParams(
            dimension_semantics=("arbitrary", "arbitrary"), vmem_limit_bytes=VMEM_LIMIT_BYTES),
        name="ada_mod",
    )(c_all, w_ada, b_ada.reshape(depth, 1, n6))


def _rows(start, n):
    return pl.ds(start if isinstance(start, int) else pl.multiple_of(start, SUBLANES), n)


def _pair_dot(x, z, low_lanes):
    zero = jnp.zeros_like(z)
    return _dot(x, jnp.concatenate([jnp.where(low_lanes, z, zero), jnp.where(low_lanes, zero, z)], axis=0))


def _tri_inverse(mats, masks, tick):
    eye, m8, m16, m32, low = masks
    a8 = [jnp.where(m8, a, 0.0) for a in mats]
    a8b = [m.astype(BF16) for m in a8]
    a2b = [_pair_dot(b, b, low).astype(BF16) for b in a8b]
    tick()
    a4 = [_pair_dot(b, b, low).astype(BF16) for b in a2b]
    p = [eye - m for m in a8]
    p = [m + _pair_dot(m.astype(BF16), b, low) for m, b in zip(p, a2b)]
    tick()
    x = [m + _pair_dot(m.astype(BF16), b, low) for m, b in zip(p, a4)]
    tick()
    prev = m8
    for m in (m16, m32, None):
        sel = jnp.logical_not(prev) if m is None else jnp.logical_and(m, jnp.logical_not(prev))
        xb = [v.astype(BF16) for v in x]
        ex = [_pair_dot(jnp.where(sel, a, 0.0).astype(BF16), b, low).astype(BF16) for a, b in zip(mats, xb)]
        tick()
        x = [v - _pair_dot(b, e, low) for v, b, e in zip(x, xb, ex)]
        tick()
        prev = m
    return x


def _mixer_kernel(dims, x_ref, xn_ref, mod_ref, n1g_ref, win_ref, caw_ref, cbw_ref, cbb_ref, lng_ref, lnb_ref,
                  cqw_ref, alog_ref, dtb_ref, gng_ref, wout_ref, sta_ref, stb_ref, stq_ref, sts_ref,
                  o_ref, na_ref, nb_ref, nq_ref, ns_ref,
                  abuf, ba_scr, bbuf, qbuf, zg_scr, gb_scr, f_qkv, f_yab, f_gam):
    sb, tt, n_tiles, d, d_a, d_b, d_c, heads, taps_a, taps_b, taps_q = dims
    dk = d_c // heads
    o_b = 3 * d_a
    o_q = o_b + 2 * d_b
    o_g = o_q + 3 * d_c
    o_s = o_g + d_c
    cpt = tt // CHUNK
    grp = min(sb, STREAMS_PER_STEP)
    assert sb % grp == 0
    n_sg = sb // grp
    pairs = [(j, hd) for j in range(grp) for hd in range(heads)]
    hpairs = [(j, hp) for j in range(grp) for hp in range(heads // 2)]
    t = pl.program_id(1)
    n_t = pl.num_programs(1)
    par = lax.rem(t, 2)

    n1g = n1g_ref[...]
    caw = caw_ref[...]
    cbw = cbw_ref[...]
    cqw = cqw_ref[...]
    gng = gng_ref[...]
    assert 2 * CHUNK == LANES and heads % 2 == 0
    ri = lax.broadcasted_iota(jnp.int32, (CHUNK, LANES), 0)
    lane = lax.broadcasted_iota(jnp.int32, (CHUNK, LANES), 1)
    low = lane < CHUNK
    ci = jnp.where(low, lane, lane - CHUNK)
    incl = ri >= ci
    strict = ri > ci
    eye = jnp.where(ri == ci, 1.0, 0.0).astype(F32)
    masks = (eye,) + tuple((ri // b) == (ci // b) for b in (8, 16, 32)) + (low,)
    row_idx = lax.broadcasted_iota(jnp.int32, (CHUNK, LANES), 0)
    is_g_lane = lax.broadcasted_iota(jnp.int32, (CHUNK, LANES), 1) < heads

    def adaln(x, m):
        return ((x * _rms_scale(x) * n1g) * (1.0 + m[:, d:2 * d]) + m[:, 0:d]).astype(BF16)

    def proj_pieces(hb, dst, places, n):
        def put(ref, halo, vals, cols=None):
            for s, r0, off in places:
                if cols is None:
                    ref[dst, s, _rows(halo + r0, n), :] = vals[off:off + n]
                else:
                    ref[dst, s, _rows(halo + r0, n), cols[0]:cols[1]] = vals[off:off + n]

        tile = 2 * LANES

        def piece_a():
            h_a = _dot(hb, win_ref[:, 0:d_a])
            c_a = _dot(hb, win_ref[:, 2 * d_a:3 * d_a])
            put(abuf, HALO_A, c_a * h_a)

        def piece_ba():
            put(ba_scr, 0, _dot(hb, win_ref[:, d_a:2 * d_a]))

        def piece_b():
            z = _dot(hb, win_ref[:, o_b:o_q])
            put(bbuf, HALO_B, z[:, 0:d_b] * _sigmoid(z[:, d_b:2 * d_b]))

        def piece_cols(ref, halo, base, c0):
            def run():
                put(ref, halo, _dot(hb, win_ref[:, c0:c0 + tile]), (c0 - base, c0 - base + tile))
            return run

        def piece_s():
            z = _dot(hb, win_ref[:, o_s:o_s + LANES])
            lane = lax.broadcasted_iota(jnp.int32, z.shape, 1)
            g_all = -jnp.exp(alog_ref[...]) * _softplus(z + dtb_ref[...])
            put(gb_scr, 0, jnp.where(lane < heads, g_all, _sigmoid(z)))

        pieces = [piece_a, piece_ba, piece_b]
        pieces += [piece_cols(qbuf, HALO_Q, o_q, c0) for c0 in range(o_q, o_g, tile)]
        pieces += [piece_cols(zg_scr, 0, o_g, c0) for c0 in range(o_g, o_s, tile)]
        return pieces + [piece_s]

    def front_pieces(src, s, r0, slot):
        def load_b():
            return bbuf[src, s, _rows(r0, CHUNK + HALO_B), :]
        part = {}

        def conv_b_low():
            part["b"] = _causal_taps(load_b(), cbw, HALO_B, range(0, SUBLANES // 2))

        def conv_ab():
            conv = part["b"] + _causal_taps(load_b(), cbw, HALO_B, range(SUBLANES // 2, SUBLANES)) + cbb_ref[...]
            xc = conv - jnp.mean(conv, axis=-1, keepdims=True)
            ln = xc * lax.rsqrt(jnp.mean(xc * xc, axis=-1, keepdims=True) + EPS) * lng_ref[...] + lnb_ref[...]
            conv = _causal_taps(abuf[src, s, _rows(r0, CHUNK + HALO_A), :], caw, HALO_A)
            y_a = ba_scr[src, s, _rows(r0, CHUNK), :] * conv
            f_yab[s, slot] = jnp.concatenate([y_a, _silu(ln)], axis=-1).astype(BF16)

        def conv_qkv(j):
            def run():
                c0 = j * d_c
                win = qbuf[src, s, _rows(r0, CHUNK + HALO_Q), c0:c0 + d_c]
                act = _silu(_causal_taps(win, cqw[:, c0:c0 + d_c], HALO_Q))
                if j < 2:
                    scale = (dk ** -0.5) if j == 0 else 1.0
                    cols = [act[:, hd * dk:(hd + 1) * dk] for hd in range(heads)]
                    act = jnp.concatenate(
                        [m * (lax.rsqrt(jnp.sum(m * m, axis=-1, keepdims=True) + EPS) * scale) for m in cols], axis=-1)
                f_qkv[s, slot, :, c0:c0 + d_c] = act
            return run

        def decay_sum():
            gbv = gb_scr[src, s, _rows(r0, CHUNK), :]
            f_gam[s, slot] = _cumsum_rows(jnp.where(is_g_lane, gbv, 0.0), row_idx)

        return [conv_b_low, conv_ab, conv_qkv(0), conv_qkv(1), conv_qkv(2), decay_sum]

    def back_load(streams, r0, slot):
        return dict(
            s_old=[ns_ref[streams[j], hd] for j, hd in pairs],
            qkv=[f_qkv[s, slot] for s in streams],
            gam=[f_gam[s, slot] for s in streams],
            yab=[f_yab[s, slot] for s in streams],
            gb=[gb_scr[par, s, _rows(r0, CHUNK), :] for s in streams],
            zg=[zg_scr[par, s, _rows(r0, CHUNK), :] for s in streams],
        )

    def back_compute(v_in, tick):
        s_old, qkv, gam, gb, zg = v_in["s_old"], v_in["qkv"], v_in["gam"], v_in["gb"], v_in["zg"]
        gam_t = [g.T for g in gam]
        q = [qkv[j][:, hd * dk:(hd + 1) * dk] for j, hd in pairs]
        k = [qkv[j][:, d_c + hd * dk:d_c + (hd + 1) * dk] for j, hd in pairs]
        v = [qkv[j][:, 2 * d_c + hd * dk:2 * d_c + (hd + 1) * dk] for j, hd in pairs]
        g_col = [gam[j][:, hd:hd + 1] for j, hd in pairs]
        g_last = [gam[j][CHUNK - 1:CHUNK, hd:hd + 1] for j, hd in pairs]
        beta = [gb[j][:, heads + hd:heads + hd + 1] for j, hd in pairs]
        eg = [jnp.exp(gc) for gc in g_col]

        def both(vals, j, hp):
            return jnp.where(low, vals[j * heads + 2 * hp], vals[j * heads + 2 * hp + 1])
        g_row2 = [jnp.concatenate([gam_t[j][2 * hp:2 * hp + 1, :], gam_t[j][2 * hp + 1:2 * hp + 2, :]], axis=-1)
                  for j, hp in hpairs]
        decay2 = [jnp.where(incl, jnp.exp(jnp.minimum(both(g_col, j, hp) - gr, 0.0)), 0.0)
                  for (j, hp), gr in zip(hpairs, g_row2)]
        zero_k = jnp.zeros((CHUNK, dk), BF16)
        k2b = [qkv[j][:, d_c + 2 * hp * dk:d_c + (2 * hp + 2) * dk].astype(BF16) for j, hp in hpairs]
        q2b = [qkv[j][:, 2 * hp * dk:(2 * hp + 2) * dk].astype(BF16) for j, hp in hpairs]
        kbd = [jnp.concatenate([jnp.concatenate([m[:, 0:dk], zero_k], axis=-1),
                                jnp.concatenate([zero_k, m[:, dk:2 * dk]], axis=-1)], axis=0) for m in k2b]
        kk2 = [_dot(m, bd, NT) for m, bd in zip(k2b, kbd)]
        qk2 = [_dot(m, bd, NT) for m, bd in zip(q2b, kbd)]
        tick()
        a_low = [jnp.where(strict, both(beta, j, hp) * m * dc, 0.0) for (j, hp), m, dc in zip(hpairs, kk2, decay2)]
        sb16 = [m.astype(BF16) for m in s_old]
        k_s = [_dot((kx * e).astype(BF16), st) for kx, e, st in zip(k, eg, sb16)]
        q_s = [_dot((m * e).astype(BF16), st) for m, e, st in zip(q, eg, sb16)]
        t_inv = [m.astype(BF16) for m in _tri_inverse(a_low, masks, tick)]
        qkd = [(m * dc).astype(BF16) for m, dc in zip(qk2, decay2)]

        def unpacked(packed, rhs):
            out = []
            for (j, hd), r in zip(pairs, rhs):
                zero = jnp.zeros_like(r)
                stacked = [r, zero] if hd % 2 == 0 else [zero, r]
                out.append(_dot(packed[j * (heads // 2) + hd // 2], jnp.concatenate(stacked, axis=0)))
            return out
        rhs = [(b * (vv - ks)).astype(BF16) for vv, ks, b in zip(v, k_s, beta)]
        vb = [m.astype(BF16) for m in unpacked(t_inv, rhs)]
        tick()
        tick()
        o_in = unpacked(qkd, vb)
        k_v = [_dot((kx * jnp.exp(gl - gc)).astype(BF16), vn, TN) for kx, gl, gc, vn in zip(k, g_last, g_col, vb)]
        tick()
        s_new = [st * jnp.exp(gl) + kv for st, gl, kv in zip(s_old, g_last, k_v)]
        o = [a + b for a, b in zip(q_s, o_in)]
        y_c = [m * _rms_scale(m) * gng * _silu(zg[j][:, hd * dk:(hd + 1) * dk]) for m, (j, hd) in zip(o, pairs)]
        return s_new, y_c

    @pl.when(t == 0)
    def _first_tile():
        abuf[0, :, HALO_A - (taps_a - 1):HALO_A, :] = sta_ref[...]
        bbuf[0, :, HALO_B - (taps_b - 1):HALO_B, :] = stb_ref[...]
        qbuf[0, :, HALO_Q - (taps_q - 1):HALO_Q, :] = stq_ref[...]
        ns_ref[...] = sts_ref[...]
        hb = jnp.concatenate([adaln(x_ref[s], mod_ref[s]) for s in range(sb)], axis=0)
        for piece in proj_pieces(hb, 0, [(s, 0, s * tt) for s in range(sb)], tt):
            piece()

        def first_front(s, carry):
            for piece in front_pieces(0, s, 0, 0):
                piece()
            return carry
        lax.fori_loop(0, sb, first_front, 0)

    abuf[1 - par, :, 0:HALO_A, :] = abuf[par, :, tt:tt + HALO_A, :]
    bbuf[1 - par, :, 0:HALO_B, :] = bbuf[par, :, tt:tt + HALO_B, :]
    qbuf[1 - par, :, 0:HALO_Q, :] = qbuf[par, :, tt:tt + HALO_Q, :]

    def chunk_step(i, carry):
        if n_sg == 1:
            sg, c = 0, i
        elif cpt == 1:
            sg, c = i, 0
        else:
            sg, c = i // cpt, lax.rem(i, cpt)
        streams = [sg * grp + j for j in range(grp)]
        r0 = c * CHUNK
        slot = lax.rem(t * cpt + c, 2)
        if cpt == 1:
            c_next, src_next = 0, 1 - par
        else:
            wrap = (c + 1) == cpt
            c_next, src_next = jnp.where(wrap, 0, c + 1), jnp.where(wrap, 1 - par, par)

        v_in = back_load(streams, r0, slot)
        hb = jnp.concatenate([adaln(xn_ref[s, _rows(r0, CHUNK), :], mod_ref[s]) for s in streams], axis=0)
        pieces = proj_pieces(hb, 1 - par, [(s, r0, j * CHUNK) for j, s in enumerate(streams)], CHUNK)
        fronts = [piece for s in streams for piece in front_pieces(src_next, s, c_next * CHUNK, 1 - slot)]
        if n_tiles == 1:
            pieces = []
            if cpt == 1:
                fronts = []
        if cpt == 1:
            for piece in pieces:
                piece()
            pieces = []
        todo_front, todo_proj = iter(fronts), iter(pieces)

        def tick():
            for todo in (todo_front, todo_proj):
                piece = next(todo, None)
                if piece is not None:
                    piece()
        s_new, y_c = back_compute(v_in, tick)
        for piece in todo_proj:
            piece()
        for piece in todo_front:
            piece()

        for (j, hd), m in zip(pairs, s_new):
            ns_ref[streams[j], hd] = m
        ymix = jnp.concatenate(
            [jnp.concatenate([v_in["yab"][j]] + [m.astype(BF16) for m in y_c[j * heads:(j + 1) * heads]], axis=-1)
             for j in range(grp)], axis=0)
        y = _dot(ymix, wout_ref[...])
        for j, s in enumerate(streams):
            g1 = mod_ref[s][:, 2 * d:3 * d]
            o_ref[s, _rows(r0, CHUNK), :] = x_ref[s, _rows(r0, CHUNK), :] + g1 * y[j * CHUNK:(j + 1) * CHUNK]
        return carry

    lax.fori_loop(0, n_sg * cpt, chunk_step, 0)

    @pl.when(t == n_t - 1)
    def _store_state():
        na_ref[...] = abuf[par, :, tt + HALO_A - (taps_a - 1):tt + HALO_A, :]
        nb_ref[...] = bbuf[par, :, tt + HALO_B - (taps_b - 1):tt + HALO_B, :]
        nq_ref[...] = qbuf[par, :, tt + HALO_Q - (taps_q - 1):tt + HALO_Q, :]


def _tiles(n_streams, t_len):
    grp = STREAMS_PER_STEP
    while n_streams % grp or MAX_TILE_ROWS // grp < CHUNK:
        grp //= 2
    tt = min(t_len, MAX_TILE_ROWS // grp)
    assert tt % CHUNK == 0 and t_len % tt == 0
    sb = max(grp, min(n_streams, MAX_TILE_ROWS // tt))
    while n_streams % sb or sb % grp:
        sb -= 1
    return sb, tt


def _full(shape, single_buffer=False):
    if single_buffer:
        return pl.BlockSpec(shape, lambda b, t: (0,) * len(shape), pipeline_mode=pl.Buffered(1))
    return pl.BlockSpec(shape, lambda b, t: (0,) * len(shape))


def _layer_of(stacked, l, single_buffer=False):
    shape = stacked.shape[1:]
    index_map = lambda b, t: (l,) + (0,) * len(shape)
    if single_buffer:
        return pl.BlockSpec((None,) + shape, index_map, pipeline_mode=pl.Buffered(1))
    return pl.BlockSpec((None,) + shape, index_map)


def _mixer_call(x, mod, p, l, st_a, st_b, st_q, st_s):
    n_b, t_len, d = x.shape
    d_a, d_b, d_c3 = p["conv_a_w"].shape[-1], p["conv_b_w"].shape[-1], p["conv_qkv_w"].shape[-1]
    d_c = d_c3 // 3
    heads = st_s.shape[2]
    taps_a, taps_b, taps_q = p["conv_a_w"].shape[0], p["conv_b_w"].shape[0], p["conv_qkv_w"].shape[0]
    assert taps_a - 1 <= HALO_A and taps_b - 1 <= HALO_B and taps_q - 1 <= HALO_Q
    assert d_c // heads == LANES and 2 * heads <= LANES
    sb, tt = _tiles(n_b, t_len)
    n_t = t_len // tt
    dims = (sb, tt, n_t, d, d_a, d_b, d_c, heads, taps_a, taps_b, taps_q)
    win, wout = p["w_in_pad"], p["w_out_bf"]

    def per_stream(shape):
        return pl.BlockSpec((sb,) + shape, lambda b, t: (b,) + (0,) * len(shape))

    def layer_state(shape):
        return pl.BlockSpec((None, sb) + shape, lambda b, t: (l, b) + (0,) * len(shape))

    x_spec = pl.BlockSpec((sb, tt, d), lambda b, t: (b, t, 0))
    x_next_spec = pl.BlockSpec((sb, tt, d), lambda b, t: (b, jnp.minimum(t + 1, n_t - 1), 0))
    state_shapes = [st_a.shape[2:], st_b.shape[2:], st_q.shape[2:], st_s.shape[2:]]
    in_specs = [x_spec, x_next_spec, per_stream((1, mod.shape[-1])), _full((1, d)), _layer_of(win, l, True),
                _full(p["conv_a_w"].shape), _full(p["conv_b_w"].shape), _full((1, d_b)), _full((1, d_b)),
                _full((1, d_b)), _full(p["conv_qkv_w"].shape), _full((1, LANES)), _full((1, LANES)),
                _full((1, LANES)), _layer_of(wout, l, True)] + [layer_state(s) for s in state_shapes]
    out_specs = [x_spec] + [per_stream(s) for s in state_shapes]
    out_shape = [jax.ShapeDtypeStruct(x.shape, F32)] + [jax.ShapeDtypeStruct(a.shape[1:], F32) for a in (st_a, st_b, st_q, st_s)]
    scratch = [
        pltpu.VMEM((2, sb, tt + HALO_A, d_a), F32),
        pltpu.VMEM((2, sb, tt, d_a), F32),
        pltpu.VMEM((2, sb, tt + HALO_B, d_b), F32),
        pltpu.VMEM((2, sb, tt + HALO_Q, d_c3), F32),
        pltpu.VMEM((2, sb, tt, d_c), F32),
        pltpu.VMEM((2, sb, tt, LANES), F32),
        pltpu.VMEM((sb, 2, CHUNK, d_c3), F32),
        pltpu.VMEM((sb, 2, CHUNK, d_a + d_b), BF16),
        pltpu.VMEM((sb, 2, CHUNK, LANES), F32),
    ]
    return pl.pallas_call(
        functools.partial(_mixer_kernel, dims),
        grid=(n_b // sb, n_t),
        in_specs=in_specs,
        out_specs=out_specs,
        out_shape=out_shape,
        scratch_shapes=scratch,
        compiler_params=pltpu.CompilerParams(
            dimension_semantics=("arbitrary", "arbitrary"), vmem_limit_bytes=VMEM_LIMIT_BYTES),
        name="mixer",
    )(x, x, mod, p["norm1_g"], win, p["conv_a_w"], p["conv_b_w"], p["conv_b_b"], p["ln_b_g"], p["ln_b_b"],
      p["conv_qkv_w"], p["a_log_pad"], p["dt_bias_pad"], p["gdn_norm_g"], wout, st_a, st_b, st_q, st_s)


def _ffn_kernel(dims, x_ref, mod_ref, n2g_ref, wgu_ref, wd_ref, fg_ref, o_ref):
    sb, tt, d, ff, ff_chunks, final_norm = dims
    n2g = n2g_ref[...]
    halves = [range(0, sb // 2), range(sb // 2, sb)] if sb % 2 == 0 else [range(sb)]
    hbs = []
    for half in halves:
        hs = []
        for s in half:
            x = x_ref[s]
            m = mod_ref[s]
            hs.append(((x * _rms_scale(x) * n2g) * (1.0 + m[:, 4 * d:5 * d]) + m[:, 3 * d:4 * d]).astype(BF16))
        hbs.append(jnp.concatenate(hs, axis=0))
    accs = [None] * len(halves)
    for c0, c1 in ff_chunks:
        for i, hb in enumerate(hbs):
            gate = _dot(hb, wgu_ref[:, c0:c1])
            up = _dot(hb, wgu_ref[:, ff + c0:ff + c1])
            part = _dot((_silu(gate) * up).astype(BF16), wd_ref[c0:c1, :])
            accs[i] = part if accs[i] is None else accs[i] + part
    for half, acc in zip(halves, accs):
        for j, s in enumerate(half):
            out = x_ref[s] + mod_ref[s][:, 5 * d:6 * d] * acc[j * tt:(j + 1) * tt]
            if final_norm:
                out = out * _rms_scale(out) * fg_ref[...]
            o_ref[s] = out


def _ffn_call(x, mod, p, l, final_g, final_norm):
    n_b, t_len, d = x.shape
    ff = p["w_down_bf"].shape[1]
    sb, tt = _tiles(n_b, t_len)
    n_chunks = -(-ff // 1024)
    step = -(-ff // (n_chunks * 2 * LANES)) * 2 * LANES
    ff_chunks = tuple((c0, min(c0 + step, ff)) for c0 in range(0, ff, step))
    dims = (sb, tt, d, ff, ff_chunks, final_norm)
    x_spec = pl.BlockSpec((sb, tt, d), lambda b, t: (b, t, 0))
    return pl.pallas_call(
        functools.partial(_ffn_kernel, dims),
        grid=(n_b // sb, t_len // tt),
        in_specs=[x_spec, pl.BlockSpec((sb, 1, mod.shape[-1]), lambda b, t: (b, 0, 0)), _full((1, d)),
                  _layer_of(p["w_gate_up_bf"], l, True), _layer_of(p["w_down_bf"], l, True), _full((1, d))],
        out_specs=x_spec,
        out_shape=jax.ShapeDtypeStruct(x.shape, F32),
        compiler_params=pltpu.CompilerParams(
            dimension_semantics=("arbitrary", "arbitrary"), vmem_limit_bytes=VMEM_LIMIT_BYTES),
        name="ffn",
    )(x, mod, p["norm2_g"], p["w_gate_up_bf"], p["w_down_bf"], final_g)


def _layer_params(l, heads, stacked, norm1_g, conv_a_w, conv_b_w, conv_b_b, ln_b_g, ln_b_b, conv_qkv_w, a_log,
                  dt_bias, gdn_norm_g, norm2_g):
    d = norm1_g.shape[-1]
    pad_small = lambda v: jnp.pad(v[l], (0, LANES - heads)).reshape(1, LANES)
    return {
        **stacked,
        "norm1_g": norm1_g[l].reshape(1, d),
        "conv_a_w": conv_a_w[l], "conv_b_w": conv_b_w[l], "conv_b_b": conv_b_b[l].reshape(1, -1),
        "ln_b_g": ln_b_g[l].reshape(1, -1), "ln_b_b": ln_b_b[l].reshape(1, -1), "conv_qkv_w": conv_qkv_w[l],
        "a_log_pad": pad_small(a_log), "dt_bias_pad": pad_small(dt_bias),
        "gdn_norm_g": gdn_norm_g[l].reshape(1, -1), "norm2_g": norm2_g[l].reshape(1, d),
    }


def _trunk(x, mods, layers, states, final_g):
    depth = len(layers)
    new = [[], [], [], []]
    for l, p in enumerate(layers):
        mod = mods[l][:, None, :]
        outs = _mixer_call(x, mod, p, l, *states)
        x = _ffn_call(outs[0], mod, p, l, final_g, l == depth - 1)
        for acc, o in zip(new, outs[1:]):
            acc.append(o)
    return (x,) + tuple(jnp.stack(n) for n in new)


def kernel(x_prompt, x_sample, state_conv_a, state_conv_b, state_conv_qkv, state_gdn, c_prompt, c_sample, norm1_g, w_ada, b_ada, w_in, conv_a_w, conv_b_w, conv_b_b, ln_b_g, ln_b_b, conv_qkv_w, a_log, dt_bias, gdn_norm_g, w_out, norm2_g, w_gate_up, w_down, final_norm_g):
    depth = w_in.shape[0]
    heads = a_log.shape[-1]
    bp = x_prompt.shape[0]
    stacked = {
        "w_in_pad": jnp.pad(w_in, ((0, 0), (0, 0), (0, LANES - 2 * heads))).astype(BF16),
        "w_out_bf": w_out.astype(BF16), "w_gate_up_bf": w_gate_up.astype(BF16), "w_down_bf": w_down.astype(BF16),
    }
    layers = [_layer_params(l, heads, stacked, norm1_g, conv_a_w, conv_b_w, conv_b_b, ln_b_g, ln_b_b, conv_qkv_w,
                            a_log, dt_bias, gdn_norm_g, norm2_g) for l in range(depth)]
    mods = _ada_call(jnp.concatenate([c_prompt, c_sample], axis=0), w_ada, b_ada)
    final_g = final_norm_g.reshape(1, -1)
    zero_states = tuple(jnp.zeros((depth, bp) + st.shape[2:], F32)
                        for st in (state_conv_a, state_conv_b, state_conv_qkv, state_gdn))
    y_p, pa, pb, pq, ps = _trunk(x_prompt, mods[:, :bp], layers, zero_states, final_g)
    y_s, sa, sb, sq, ss = _trunk(x_sample, mods[:, bp:], layers,
                                 (state_conv_a, state_conv_b, state_conv_qkv, state_gdn), final_g)
    return (y_p, y_s, pa, pb, pq, ps, sa, sb, sq, ss)
```
